```python
import math, functools
import jax, jax.numpy as jnp
from jax import lax
import numpy as np

D_MODEL = 2048
BATCH = 4
SEQ = 2048
DEPTH = 4
DEC_BATCH = 8
DEC_SEQ = 8
PAST_LEN = 16384
PAGE_SIZE = 128

SSD_HEADS = 12
SSD_HEAD_DIM = 64
SSD_INNER = SSD_HEADS * SSD_HEAD_DIM
SSD_GROUPS = 2
SSD_STATE = 128
SSD_CONV = 4
SSD_CHUNK = 128
SSD_CONV_DIM = SSD_INNER + 2 * SSD_GROUPS * SSD_STATE
CC_CH = 512
CC_WIDTH = 31
FOX_HEADS = 12
FOX_HEAD_DIM = 64
FOX_WIDTH = FOX_HEADS * FOX_HEAD_DIM
Q_BLOCK = 128
MIX_WIDTH = SSD_INNER + CC_CH + FOX_WIDTH
IN_SIZES = (SSD_INNER, SSD_CONV_DIM, SSD_HEADS, 2 * CC_CH, 3 * FOX_WIDTH, FOX_HEADS)
N_IN = SSD_INNER + SSD_CONV_DIM + SSD_HEADS + 2 * CC_CH + 3 * FOX_WIDTH + FOX_HEADS
MEM_LEN = 256
XA_HEADS = 4
XA_HEAD_DIM = 128
XA_WIDTH = XA_HEADS * XA_HEAD_DIM
D_FF = -(-8 * D_MODEL // (3 * 256)) * 256
EPS = 1e-6
FG_BIAS_INIT = 3.0

kernel_name = 'hymba_ssd_conformer_fox_decoder_step'


def rms_norm(x, g):
    xf = x.astype(jnp.float32)
    y = xf * lax.rsqrt(jnp.mean(xf * xf, axis=-1, keepdims=True) + EPS)
    return (y * g.astype(jnp.float32)).astype(x.dtype)


def layer_norm(x, g, b):
    xf = x.astype(jnp.float32)
    xc = xf - jnp.mean(xf, axis=-1, keepdims=True)
    y = xc * lax.rsqrt(jnp.mean(xc * xc, axis=-1, keepdims=True) + EPS)
    return (y * g.astype(jnp.float32) + b.astype(jnp.float32)).astype(x.dtype)


def split_cols(u, sizes):
    idx = [int(i) for i in np.cumsum(sizes)[:-1]]
    return jnp.split(u, idx, axis=-1)


def causal_dwconv(x_ext, w, b):
    y = lax.conv_general_dilated(x_ext, w[:, None, :].astype(x_ext.dtype), (1,), 'VALID',
                                 dimension_numbers=('NWC', 'WIO', 'NWC'),
                                 feature_group_count=x_ext.shape[-1])
    return y + b.astype(y.dtype)


def ssd_scan(x, dt, a, bm, cm, h0):
    f32 = jnp.float32
    b, l, nh, p = x.shape
    g, n = bm.shape[2], bm.shape[3]
    hg = nh // g
    q = math.gcd(l, SSD_CHUNK)
    c = l // q
    x = x.astype(f32).reshape(b, c, q, g, hg, p)
    dt = dt.reshape(b, c, q, g, hg)
    bm = bm.astype(f32).reshape(b, c, q, g, n)
    cm = cm.astype(f32).reshape(b, c, q, g, n)
    cum = jnp.cumsum(dt * a.reshape(g, hg), axis=2)
    seg = cum[:, :, :, None] - cum[:, :, None, :]
    tri = jnp.tril(jnp.ones((q, q), dtype=bool))[:, :, None, None]
    decay = jnp.exp(jnp.where(tri, seg, -jnp.inf))
    cb = jnp.einsum('bctgn,bcsgn->bctsg', cm, bm)
    m = cb[..., None] * decay * dt[:, :, None]
    y_diag = jnp.einsum('bctsgh,bcsghp->bctghp', m, x)
    w_s = jnp.exp(cum[:, :, -1:] - cum) * dt
    chunk_states = jnp.einsum('bcsgn,bcsghp->bcghpn', bm, x * w_s[..., None])
    chunk_decay = jnp.exp(cum[:, :, -1])

    def step(h, inp):
        st, dec = inp
        return dec[..., None, None] * h + st, h

    h_last, h_in = lax.scan(step, h0.astype(f32).reshape(b, g, hg, p, n),
                            (jnp.moveaxis(chunk_states, 1, 0), jnp.moveaxis(chunk_decay, 1, 0)))
    h_in = jnp.moveaxis(h_in, 0, 1)
    y_off = jnp.einsum('bctgn,bcghpn->bctghp', cm, h_in) * jnp.exp(cum)[..., None]
    y = (y_diag + y_off).reshape(b, l, nh, p)
    return y, h_last.reshape(b, nh, p, n)


def ssd_mixer(z, xbc, dt_raw, conv_buf, h0, lp):
    f32 = jnp.float32
    ext = jnp.concatenate([conv_buf.astype(xbc.dtype), xbc], axis=1)
    new_buf = ext[:, ext.shape[1] - (SSD_CONV - 1):]
    u = jax.nn.silu(causal_dwconv(ext, lp['ssd_conv_w'], lp['ssd_conv_b']))
    xs, bm, cm = split_cols(u, (SSD_INNER, SSD_GROUPS * SSD_STATE, SSD_GROUPS * SSD_STATE))
    b, l, _ = xs.shape
    xs = xs.reshape(b, l, SSD_HEADS, SSD_HEAD_DIM)
    bm = bm.reshape(b, l, SSD_GROUPS, SSD_STATE)
    cm = cm.reshape(b, l, SSD_GROUPS, SSD_STATE)
    dt = jax.nn.softplus(dt_raw.astype(f32) + lp['ssd_dt_bias'].astype(f32))
    a = -jnp.exp(lp['ssd_a_log'].astype(f32))
    y, h_last = ssd_scan(xs, dt, a, bm, cm, h0)
    y = y + lp['ssd_d'].astype(f32)[:, None] * xs.astype(f32)
    y = y.reshape(b, l, SSD_INNER) * jax.nn.silu(z.astype(f32))
    return rms_norm(y, lp['ssd_norm']).astype(z.dtype), new_buf, h_last.astype(h0.dtype)


def conformer_conv(glu_in, conv_buf, lp):
    a, gate = jnp.split(glu_in, 2, axis=-1)
    u = a * jax.nn.sigmoid(gate)
    ext = jnp.concatenate([conv_buf.astype(u.dtype), u], axis=1)
    new_buf = ext[:, ext.shape[1] - (CC_WIDTH - 1):]
    v = layer_norm(causal_dwconv(ext, lp['cc_dw_w'], lp['cc_dw_b']), lp['cc_ln_g'], lp['cc_ln_b'])
    return jax.nn.silu(v), new_buf


def fox_attend_block(q, k, v, cq, ck, mask):
    s = jnp.einsum('bthd,bshd->bhts', q, k, preferred_element_type=jnp.float32) * (FOX_HEAD_DIM ** -0.5)
    s = s + jnp.swapaxes(cq, 1, 2)[..., :, None] - jnp.swapaxes(ck, 1, 2)[..., None, :]
    s = jnp.where(mask, s, -jnp.inf)
    p = jax.nn.softmax(s, axis=-1)
    return jnp.einsum('bhts,bshd->bthd', p.astype(v.dtype), v)


def fox_prompt(q, k, v, lf):
    b, l, h, d = q.shape
    cum = jnp.cumsum(lf.astype(jnp.float32), axis=1)
    kpos = jnp.arange(l)

    def block(i):
        start = i * Q_BLOCK
        qb = lax.dynamic_slice_in_dim(q, start, Q_BLOCK, axis=1)
        cqb = lax.dynamic_slice_in_dim(cum, start, Q_BLOCK, axis=1)
        mask = kpos[None, :] <= (start + jnp.arange(Q_BLOCK))[:, None]
        return fox_attend_block(qb, k, v, cqb, cum, mask)

    out = lax.map(block, jnp.arange(l // Q_BLOCK))
    return jnp.moveaxis(out, 0, 1).reshape(b, l, h, d)


def fox_sample(q, k, v, lf, k_pool, v_pool, lf_pool, page_table):
    b, t, h, d = q.shape
    k_all = jnp.concatenate([k_pool[page_table].reshape(b, -1, h, d), k], axis=1)
    v_all = jnp.concatenate([v_pool[page_table].reshape(b, -1, h, d), v], axis=1)
    lf_all = jnp.concatenate([lf_pool[page_table].reshape(b, -1, h), lf], axis=1).astype(jnp.float32)
    past = k_all.shape[1] - t
    cum = jnp.cumsum(lf_all, axis=1)
    mask = jnp.arange(past + t)[None, :] <= (past + jnp.arange(t))[:, None]
    return fox_attend_block(q, k_all, v_all, cum[:, past:], cum, mask)


def mem_kv(mem, lp):
    b, m, _ = mem.shape
    mn = rms_norm(mem, lp['norm_mem'])
    k = rms_norm((mn @ lp['xa_wk']).reshape(b, m, XA_HEADS, XA_HEAD_DIM), lp['xa_k_norm'])
    v = (mn @ lp['xa_wv']).reshape(b, m, XA_HEADS, XA_HEAD_DIM)
    return k, v


def cross_attn(h, mk, mv, lp):
    b, l, _ = h.shape
    q = rms_norm((h @ lp['xa_wq']).reshape(b, l, XA_HEADS, XA_HEAD_DIM), lp['xa_q_norm'])
    s = jnp.einsum('bthd,bmhd->bhtm', q, mk, preferred_element_type=jnp.float32) * (XA_HEAD_DIM ** -0.5)
    p = jax.nn.softmax(s, axis=-1)
    o = jnp.einsum('bhtm,bmhd->bthd', p.astype(mv.dtype), mv).reshape(b, l, XA_WIDTH)
    return o @ lp['xa_wo']


def layer_forward(x, lp, ssd_buf, ssd_h0, cc_buf, fox_fn, mem_k, mem_v):
    b, l, _ = x.shape
    n = rms_norm(x, lp['norm_mix'])
    z, xbc, dt_raw, glu_in, qkv, fg = split_cols(n @ lp['w_in'], IN_SIZES)
    y_ssd, ssd_buf_new, ssd_h = ssd_mixer(z, xbc, dt_raw, ssd_buf, ssd_h0, lp)
    y_cc, cc_buf_new = conformer_conv(glu_in, cc_buf, lp)
    q, k, v = jnp.split(qkv, 3, axis=-1)
    q = rms_norm(q.reshape(b, l, FOX_HEADS, FOX_HEAD_DIM), lp['fox_q_norm'])
    k = rms_norm(k.reshape(b, l, FOX_HEADS, FOX_HEAD_DIM), lp['fox_k_norm'])
    v = v.reshape(b, l, FOX_HEADS, FOX_HEAD_DIM)
    lf = jax.nn.log_sigmoid(fg.astype(jnp.float32) + lp['fox_fg_bias'].astype(jnp.float32)).astype(x.dtype)
    y_fox = fox_fn(q, k, v, lf).reshape(b, l, FOX_WIDTH)
    x = x + jnp.concatenate([y_ssd.astype(x.dtype), y_cc.astype(x.dtype), y_fox.astype(x.dtype)], axis=-1) @ lp['w_out']
    x = x + cross_attn(rms_norm(x, lp['norm_xa']), mem_k, mem_v, lp)
    hn = rms_norm(x, lp['norm_ffn'])
    x = x + (jax.nn.silu(hn @ lp['ffn_wg']) * (hn @ lp['ffn_wu'])) @ lp['ffn_wd']
    return x, (k, v, lf, ssd_h, ssd_buf_new, cc_buf_new)


def setup_inputs(seed: int = 0) -> dict:
    key = jax.random.key(seed)
    ks = iter(jax.random.split(key, 64))
    f32 = jnp.float32

    def nrm(shape, scale=1.0):
        return jax.random.normal(next(ks), shape, f32) * scale

    def gain(shape):
        return 1.0 + nrm(shape, 0.02)

    n_pages = PAST_LEN // PAGE_SIZE
    n_pool = (5 * DEC_BATCH * n_pages) // 4
    page_table = jax.random.permutation(next(ks), n_pool)[: DEC_BATCH * n_pages].reshape(DEC_BATCH, n_pages).astype(jnp.int32)
    dt0 = jnp.exp(jax.random.uniform(next(ks), (DEPTH, SSD_HEADS), f32, math.log(1e-3), math.log(1e-1)))
    dt_bias = dt0 + jnp.log(-jnp.expm1(-dt0))
    a_log = jnp.log(jax.random.uniform(next(ks), (DEPTH, SSD_HEADS), f32, 1.0, 16.0))
    return {
        'x_prompt': nrm((BATCH, SEQ, D_MODEL)),
        'x_sample': nrm((DEC_BATCH, DEC_SEQ, D_MODEL)),
        'mem_prompt': nrm((BATCH, MEM_LEN, D_MODEL)),
        'cache_fox_k': nrm((DEPTH, n_pool, PAGE_SIZE, FOX_HEADS, FOX_HEAD_DIM)),
        'cache_fox_v': nrm((DEPTH, n_pool, PAGE_SIZE, FOX_HEADS, FOX_HEAD_DIM)),
        'cache_fox_logf': jax.nn.log_sigmoid(FG_BIAS_INIT + nrm((DEPTH, n_pool, PAGE_SIZE, FOX_HEADS))),
        'page_table': page_table,
        'state_ssd': nrm((DEPTH, DEC_BATCH, SSD_HEADS, SSD_HEAD_DIM, SSD_STATE), 0.5),
        'state_ssd_conv': nrm((DEPTH, DEC_BATCH, SSD_CONV - 1, SSD_CONV_DIM)),
        'state_cc_conv': nrm((DEPTH, DEC_BATCH, CC_WIDTH - 1, CC_CH)),
        'cache_mem_k': nrm((DEPTH, DEC_BATCH, MEM_LEN, XA_HEADS, XA_HEAD_DIM)),
        'cache_mem_v': nrm((DEPTH, DEC_BATCH, MEM_LEN, XA_HEADS, XA_HEAD_DIM)),
        'norm_mix': gain((DEPTH, D_MODEL)),
        'w_in': nrm((DEPTH, D_MODEL, N_IN), D_MODEL ** -0.5),
        'ssd_conv_w': nrm((DEPTH, SSD_CONV, SSD_CONV_DIM), SSD_CONV ** -0.5),
        'ssd_conv_b': nrm((DEPTH, SSD_CONV_DIM), 0.02),
        'ssd_dt_bias': dt_bias,
        'ssd_a_log': a_log,
        'ssd_d': 1.0 + nrm((DEPTH, SSD_HEADS), 0.1),
        'ssd_norm': gain((DEPTH, SSD_INNER)),
        'cc_dw_w': nrm((DEPTH, CC_WIDTH, CC_CH), CC_WIDTH ** -0.5),
        'cc_dw_b': nrm((DEPTH, CC_CH), 0.02),
        'cc_ln_g': gain((DEPTH, CC_CH)),
        'cc_ln_b': nrm((DEPTH, CC_CH), 0.02),
        'fox_q_norm': gain((DEPTH, FOX_HEAD_DIM)),
        'fox_k_norm': gain((DEPTH, FOX_HEAD_DIM)),
        'fox_fg_bias': FG_BIAS_INIT + nrm((DEPTH, FOX_HEADS), 0.1),
        'w_out': nrm((DEPTH, MIX_WIDTH, D_MODEL), MIX_WIDTH ** -0.5),
        'norm_xa': gain((DEPTH, D_MODEL)),
        'norm_mem': gain((DEPTH, D_MODEL)),
        'xa_wq': nrm((DEPTH, D_MODEL, XA_WIDTH), D_MODEL ** -0.5),
        'xa_wk': nrm((DEPTH, D_MODEL, XA_WIDTH), D_MODEL ** -0.5),
        'xa_wv': nrm((DEPTH, D_MODEL, XA_WIDTH), D_MODEL ** -0.5),
        'xa_q_norm': gain((DEPTH, XA_HEAD_DIM)),
        'xa_k_norm': gain((DEPTH, XA_HEAD_DIM)),
        'xa_wo': nrm((DEPTH, XA_WIDTH, D_MODEL), XA_WIDTH ** -0.5),
        'norm_ffn': gain((DEPTH, D_MODEL)),
        'ffn_wg': nrm((DEPTH, D_MODEL, D_FF), D_MODEL ** -0.5),
        'ffn_wu': nrm((DEPTH, D_MODEL, D_FF), D_MODEL ** -0.5),
        'ffn_wd': nrm((DEPTH, D_FF, D_MODEL), D_FF ** -0.5),
    }


def reference(x_prompt, x_sample, mem_prompt, cache_fox_k, cache_fox_v, cache_fox_logf, page_table,
              state_ssd, state_ssd_conv, state_cc_conv, cache_mem_k, cache_mem_v,
              norm_mix, w_in, ssd_conv_w, ssd_conv_b, ssd_dt_bias, ssd_a_log, ssd_d, ssd_norm,
              cc_dw_w, cc_dw_b, cc_ln_g, cc_ln_b, fox_q_norm, fox_k_norm, fox_fg_bias, w_out,
              norm_xa, norm_mem, xa_wq, xa_wk, xa_wv, xa_q_norm, xa_k_norm, xa_wo,
              norm_ffn, ffn_wg, ffn_wu, ffn_wd):
    dtype = x_prompt.dtype
    bp = x_prompt.shape[0]
    hp, hs = x_prompt, x_sample
    p_k, p_v, p_lf, p_ssd, p_ssdc, p_cc, p_mk, p_mv = [], [], [], [], [], [], [], []
    s_k, s_v, s_lf, s_ssd, s_ssdc, s_cc = [], [], [], [], [], []
    for i in range(DEPTH):
        lp = {
            'norm_mix': norm_mix[i], 'w_in': w_in[i], 'ssd_conv_w': ssd_conv_w[i], 'ssd_conv_b': ssd_conv_b[i],
            'ssd_dt_bias': ssd_dt_bias[i], 'ssd_a_log': ssd_a_log[i], 'ssd_d': ssd_d[i], 'ssd_norm': ssd_norm[i],
            'cc_dw_w': cc_dw_w[i], 'cc_dw_b': cc_dw_b[i], 'cc_ln_g': cc_ln_g[i], 'cc_ln_b': cc_ln_b[i],
            'fox_q_norm': fox_q_norm[i], 'fox_k_norm': fox_k_norm[i], 'fox_fg_bias': fox_fg_bias[i],
            'w_out': w_out[i], 'norm_xa': norm_xa[i], 'norm_mem': norm_mem[i], 'xa_wq': xa_wq[i],
            'xa_wk': xa_wk[i], 'xa_wv': xa_wv[i], 'xa_q_norm': xa_q_norm[i], 'xa_k_norm': xa_k_norm[i],
            'xa_wo': xa_wo[i], 'norm_ffn': norm_ffn[i], 'ffn_wg': ffn_wg[i], 'ffn_wu': ffn_wu[i], 'ffn_wd': ffn_wd[i],
        }
        mk, mv = mem_kv(mem_prompt, lp)
        hp, st = layer_forward(
            hp, lp,
            jnp.zeros((bp, SSD_CONV - 1, SSD_CONV_DIM), dtype),
            jnp.zeros((bp, SSD_HEADS, SSD_HEAD_DIM, SSD_STATE), dtype),
            jnp.zeros((bp, CC_WIDTH - 1, CC_CH), dtype),
            fox_prompt, mk, mv)
        p_k.append(st[0]); p_v.append(st[1]); p_lf.append(st[2])
        p_ssd.append(st[3]); p_ssdc.append(st[4]); p_cc.append(st[5])
        p_mk.append(mk); p_mv.append(mv)
        fox_fn = functools.partial(fox_sample, k_pool=cache_fox_k[i], v_pool=cache_fox_v[i],
                                   lf_pool=cache_fox_logf[i], page_table=page_table)
        hs, st = layer_forward(hs, lp, state_ssd_conv[i], state_ssd[i], state_cc_conv[i],
                               fox_fn, cache_mem_k[i], cache_mem_v[i])
        s_k.append(st[0]); s_v.append(st[1]); s_lf.append(st[2])
        s_ssd.append(st[3]); s_ssdc.append(st[4]); s_cc.append(st[5])
    return (hp, hs,
            jnp.stack(p_k), jnp.stack(p_v), jnp.stack(p_lf), jnp.stack(p_ssd), jnp.stack(p_ssdc),
            jnp.stack(p_cc), jnp.stack(p_mk), jnp.stack(p_mv),
            jnp.stack(s_k), jnp.stack(s_v), jnp.stack(s_lf), jnp.stack(s_ssd), jnp.stack(s_ssdc),
            jnp.stack(s_cc))
```

```python
import functools

import jax
import jax.numpy as jnp
from jax import lax
from jax.experimental import pallas as pl
from jax.experimental.pallas import tpu as pltpu

F32 = jnp.float32
BF16 = jnp.bfloat16
HI = lax.Precision.HIGHEST

D_MODEL = 2048
SSD_HEADS = 12
SSD_HEAD_DIM = 64
SSD_INNER = SSD_HEADS * SSD_HEAD_DIM
SSD_GROUPS = 2
SSD_STATE = 128
SSD_CONV = 4
SSD_CHUNK = 128
SSD_CONV_DIM = SSD_INNER + 2 * SSD_GROUPS * SSD_STATE
CC_CH = 512
CC_WIDTH = 31
FOX_HEADS = 12
FOX_HEAD_DIM = 64
FOX_WIDTH = FOX_HEADS * FOX_HEAD_DIM
PAGE_SIZE = 128
XA_HEADS = 4
XA_HEAD_DIM = 128
XA_WIDTH = XA_HEADS * XA_HEAD_DIM
EPS = 1e-6

LANES = 128
HEAD_ROWS = 16
VMEM_LIMIT_BYTES = 56 * 1024 * 1024
PAGES_PER_STEP = 8

_OFF_Z = 0
_OFF_XBC = _OFF_Z + SSD_INNER
_OFF_DT = _OFF_XBC + SSD_CONV_DIM
_OFF_GLU = _OFF_DT + SSD_HEADS
_OFF_Q = _OFF_GLU + 2 * CC_CH
_OFF_K = _OFF_Q + FOX_WIDTH
_OFF_V = _OFF_K + FOX_WIDTH
_OFF_FG = _OFF_V + FOX_WIDTH


def _params(*sem):
    return pltpu.CompilerParams(dimension_semantics=sem, vmem_limit_bytes=VMEM_LIMIT_BYTES)


def _resident(shape):
    nd = len(shape)
    return pl.BlockSpec(shape, lambda *_: (0,) * nd, pipeline_mode=pl.Buffered(1))


def _rms_rows(x, g):
    ms = jnp.mean(x * x, axis=-1, keepdims=True)
    return x * lax.rsqrt(ms + EPS) * g


def _softplus(x):
    return jnp.maximum(x, 0.0) + jnp.log1p(jnp.exp(-jnp.abs(x)))


def _silu(x):
    return x * jax.nn.sigmoid(x)


def _dot(a, b):
    return jnp.dot(a, b, preferred_element_type=F32)


def _dot_hi(a, b):
    return jnp.dot(a, b, precision=HI, preferred_element_type=F32)


def _dot_nt(a, b):
    return lax.dot_general(a, b, (((1,), (1,)), ((), ())), preferred_element_type=F32)


def _dot_tn(a, b):
    return lax.dot_general(a, b, (((0,), (0,)), ((), ())), preferred_element_type=F32)


def _norm_kernel(x_ref, g_ref, o_ref):
    o_ref[...] = _rms_rows(x_ref[...], g_ref[...]).astype(BF16)


def rmsnorm_bf16(x, g):
    m, d = x.shape
    tm = min(m, 512)
    return pl.pallas_call(
        _norm_kernel,
        out_shape=jax.ShapeDtypeStruct((m, d), BF16),
        grid=(m // tm,),
        in_specs=[pl.BlockSpec((tm, d), lambda i: (i, 0)), pl.BlockSpec((1, d), lambda i: (0, 0))],
        out_specs=pl.BlockSpec((tm, d), lambda i: (i, 0)),
        compiler_params=_params("parallel"),
        name="rmsnorm_bf16",
    )(x, g)


def _proj_ssd_kernel(n_ref, wz_ref, wx_ref, wd_ref, bd_ref, z_ref, xbc_ref, dt_ref):
    n = n_ref[...]
    z_ref[...] = _dot(n, wz_ref[...])
    xbc_ref[...] = _dot(n, wx_ref[...])
    dt_ref[...] = _softplus(_dot(n, wd_ref[...]) + bd_ref[...])


def proj_ssd(n, wz, wx, wd, bd):
    m, d = n.shape
    tm = min(m, 512)
    row = lambda w: pl.BlockSpec((tm, w), lambda i: (i, 0))
    return pl.pallas_call(
        _proj_ssd_kernel,
        out_shape=(jax.ShapeDtypeStruct((m, SSD_INNER), F32),
                   jax.ShapeDtypeStruct((m, SSD_CONV_DIM), F32),
                   jax.ShapeDtypeStruct((m, LANES), F32)),
        grid=(m // tm,),
        in_specs=[row(d), _resident(wz.shape), _resident(wx.shape), _resident(wd.shape), _resident(bd.shape)],
        out_specs=(row(SSD_INNER), row(SSD_CONV_DIM), row(LANES)),
        compiler_params=_params("parallel"),
        name="proj_ssd",
    )(n, wz, wx, wd, bd)


def _proj_ccq_kernel(n_ref, wg_ref, wq_ref, qg_ref, sel_ref, selt_ref, u_ref, q_ref):
    n = n_ref[...]
    glu = _dot(n, wg_ref[...])
    u_ref[...] = glu[:, :CC_CH] * jax.nn.sigmoid(glu[:, CC_CH:])
    q = _dot(n, wq_ref[...])
    ms = _dot_hi(q * q, selt_ref[...]) * (1.0 / FOX_HEAD_DIM)
    r = _dot_hi(lax.rsqrt(ms + EPS), sel_ref[...])
    q_ref[...] = (q * r * qg_ref[...]).astype(BF16)


def proj_ccq(n, wg, wq, qg, sel, selt):
    m, d = n.shape
    tm = min(m, 512)
    row = lambda w: pl.BlockSpec((tm, w), lambda i: (i, 0))
    return pl.pallas_call(
        _proj_ccq_kernel,
        out_shape=(jax.ShapeDtypeStruct((m, CC_CH), F32), jax.ShapeDtypeStruct((m, FOX_WIDTH), BF16)),
        grid=(m // tm,),
        in_specs=[row(d), _resident(wg.shape), _resident(wq.shape), _resident(qg.shape),
                  _resident(sel.shape), _resident(selt.shape)],
        out_specs=(row(CC_CH), row(FOX_WIDTH)),
        compiler_params=_params("parallel"),
        name="proj_ccq",
    )(n, wg, wq, qg, sel, selt)


def _proj_t_kernel(n_ref, wk_ref, wv_ref, wf_ref, gk_ref, bf_ref, kt_ref, vt_ref, lf_ref, cum_ref, carry_ref,
                   *, tm, seg):
    j = pl.program_id(1)
    n = n_ref[...]
    kt = _dot_nt(wk_ref[...], n)
    k3 = kt.reshape(FOX_HEADS, FOX_HEAD_DIM, tm)
    ms = jnp.mean(k3 * k3, axis=1, keepdims=True)
    k3 = k3 * lax.rsqrt(ms + EPS)
    kt_ref[0] = k3.reshape(FOX_WIDTH, tm) * jnp.tile(gk_ref[...], (1, tm // gk_ref.shape[1]))
    vt_ref[0] = _dot_nt(wv_ref[...], n)
    raw = _dot_nt(wf_ref[...], n) + jnp.tile(bf_ref[...], (1, tm // bf_ref.shape[1]))
    lf = -_softplus(-raw)
    lf_ref[0] = lf

    @pl.when((j * tm) % seg == 0)
    def _():
        carry_ref[...] = jnp.zeros_like(carry_ref)

    s_idx = lax.broadcasted_iota(jnp.int32, (tm, tm), 0)
    t_idx = lax.broadcasted_iota(jnp.int32, (tm, tm), 1)
    upper = (s_idx <= t_idx) & ((s_idx // seg) == (t_idx // seg))
    cum = _dot_hi(lf, upper.astype(F32))
    carry = carry_ref[...]
    cum_ref[0] = cum + jnp.tile(carry, (1, tm // LANES)) if tm >= LANES else cum + carry[:, :tm]
    carry_ref[...] = carry + _dot_hi(lf, jnp.ones((tm, LANES), F32))


def proj_t(n, wk, wv, wf, gk, bf, *, nb, seq, seg):
    m, d = n.shape
    tm = min(seq, 512)
    assert seg % tm == 0 or seq == tm
    nj = seq // tm
    lane = min(tm, LANES)
    gk = gk[:, :lane]
    bf = bf[:, :lane]
    big = lambda rows: pl.BlockSpec((1, rows, tm), lambda b, j: (b, 0, j))
    return pl.pallas_call(
        functools.partial(_proj_t_kernel, tm=tm, seg=seg),
        out_shape=(jax.ShapeDtypeStruct((nb, FOX_WIDTH, seq), F32),
                   jax.ShapeDtypeStruct((nb, FOX_WIDTH, seq), F32),
                   jax.ShapeDtypeStruct((nb, HEAD_ROWS, seq), F32),
                   jax.ShapeDtypeStruct((nb, HEAD_ROWS, seq), F32)),
        grid=(nb, nj),
        in_specs=[pl.BlockSpec((tm, d), lambda b, j: (b * nj + j, 0)),
                  _resident(wk.shape), _resident(wv.shape), _resident(wf.shape),
                  _resident(gk.shape), _resident(bf.shape)],
        out_specs=(big(FOX_WIDTH), big(FOX_WIDTH), big(HEAD_ROWS), big(HEAD_ROWS)),
        scratch_shapes=[pltpu.VMEM((HEAD_ROWS, LANES), F32)],
        compiler_params=_params("parallel", "arbitrary"),
        name="proj_t",
    )(n, wk, wv, wf, gk, bf)


def _ssd_kernel(z_ref, xbc_ref, dt_ref, buf_ref, h0_ref, cw_ref, cb_ref, alog_ref, dl_ref, gn_ref, sel_ref,
                y_ref, nbuf_ref, hl_ref, ext_ref, h_ref, yg_ref, *, nc, valid):
    c = pl.program_id(1)
    q = SSD_CHUNK
    pad = 8
    c_last = (valid - 1) // q
    t_last = (valid - 1) % q

    @pl.when(c == 0)
    def _():
        ext_ref[pl.ds(pad - 3, 3), :] = buf_ref[0]
        h_ref[...] = h0_ref[0]

    ext_ref[pl.ds(pad, q), :] = xbc_ref[...]
    u = jnp.zeros((q, SSD_CONV_DIM), F32) + cb_ref[...]
    for j in range(SSD_CONV):
        u = u + ext_ref[pl.ds(pad - 3 + j, q), :] * cw_ref[pl.ds(j, 1), :]
    u = _silu(u)

    @pl.when(c == c_last)
    def _():
        nbuf_ref[0] = ext_ref[pl.ds(pad + t_last - 2, 3), :]

    ext_ref[pl.ds(pad - 3, 3), :] = ext_ref[pl.ds(pad + q - 3, 3), :]

    row = lax.broadcasted_iota(jnp.int32, (q, LANES), 0)
    col = lax.broadcasted_iota(jnp.int32, (q, LANES), 1)
    dt = jnp.where(c * q + row < valid, dt_ref[...], 0.0)
    a = -jnp.exp(alog_ref[...])
    tril = row >= col
    cum = _dot_hi(tril.astype(F32), dt * a)
    sel = sel_ref[...]
    cumw = _dot_hi(cum, sel)
    dtw = _dot_hi(dt, sel)
    cum_t = cum.T
    dt_t = dt.T
    lastw = cumw[q - 1:q, :]
    w_in_state = jnp.exp(lastw - cumw) * dtw
    e_cumw = jnp.exp(cumw)
    e_last = jnp.exp(cum[q - 1:q, :])
    lane_lo = col < SSD_HEAD_DIM
    row_lo = row < SSD_HEAD_DIM

    hpg = SSD_HEADS // SSD_GROUPS
    ssq = jnp.zeros((q, 1), F32)
    for g in range(SSD_GROUPS):
        bm = u[:, SSD_INNER + g * SSD_STATE:SSD_INNER + (g + 1) * SSD_STATE]
        cm = u[:, SSD_INNER + (SSD_GROUPS + g) * SSD_STATE:SSD_INNER + (SSD_GROUPS + g + 1) * SSD_STATE]
        bm16 = bm.astype(BF16)
        cm16 = cm.astype(BF16)
        cb = _dot_nt(cm16, bm16)
        for pp in range(hpg // 2):
            p = g * (hpg // 2) + pp
            lo, hi = p * LANES, (p + 1) * LANES
            xs = u[:, lo:hi]
            xs16 = xs.astype(BF16)
            yd = []
            for hh in range(2):
                h = 2 * p + hh
                seg = cum[:, h:h + 1] - cum_t[h:h + 1, :]
                decay = jnp.exp(jnp.where(tril, seg, -jnp.inf))
                mm = cb * decay * dt_t[h:h + 1, :]
                yd.append(_dot(mm.astype(BF16), xs16))
            y = jnp.where(lane_lo, yd[0], yd[1])
            h_pair = h_ref[pl.ds(lo, LANES), :]
            y = y + _dot_nt(cm16, h_pair.astype(BF16)) * e_cumw[:, lo:hi]
            y = y + dl_ref[:, lo:hi] * xs
            cs = _dot_tn((xs * w_in_state[:, lo:hi]).astype(BF16), bm16)
            dec = jnp.where(row_lo, e_last[:, 2 * p:2 * p + 1], e_last[:, 2 * p + 1:2 * p + 2])
            h_ref[pl.ds(lo, LANES), :] = dec * h_pair + cs
            yg = y * _silu(z_ref[:, lo:hi])
            yg_ref[:, lo:hi] = yg
            ssq = ssq + jnp.sum(yg * yg, axis=-1, keepdims=True)
    scale = lax.rsqrt(ssq * (1.0 / SSD_INNER) + EPS)
    y_ref[...] = (yg_ref[...] * scale * gn_ref[...]).astype(BF16)

    @pl.when(c == nc - 1)
    def _():
        hl_ref[0] = h_ref[...]


def ssd_mixer(z, xbc, dt, buf, h0, cw, cb, alog, dlane, gn, sel, *, nb, seq, valid):
    q = SSD_CHUNK
    nc = seq // q
    row = lambda w: pl.BlockSpec((q, w), lambda b, c: (b * nc + c, 0))
    return pl.pallas_call(
        functools.partial(_ssd_kernel, nc=nc, valid=valid),
        out_shape=(jax.ShapeDtypeStruct((nb * seq, SSD_INNER), BF16),
                   jax.ShapeDtypeStruct((nb, SSD_CONV - 1, SSD_CONV_DIM), F32),
                   jax.ShapeDtypeStruct((nb, SSD_INNER, SSD_STATE), F32)),
        grid=(nb, nc),
        in_specs=[row(SSD_INNER), row(SSD_CONV_DIM), row(LANES),
                  pl.BlockSpec((1, SSD_CONV - 1, SSD_CONV_DIM), lambda b, c: (b, 0, 0)),
                  pl.BlockSpec((1, SSD_INNER, SSD_STATE), lambda b, c: (b, 0, 0)),
                  _resident(cw.shape), _resident(cb.shape), _resident(alog.shape), _resident(dlane.shape),
                  _resident(gn.shape), _resident(sel.shape)],
        out_specs=(row(SSD_INNER),
                   pl.BlockSpec((1, SSD_CONV - 1, SSD_CONV_DIM), lambda b, c: (b, 0, 0)),
                   pl.BlockSpec((1, SSD_INNER, SSD_STATE), lambda b, c: (b, 0, 0))),
        scratch_shapes=[pltpu.VMEM((q + 8, SSD_CONV_DIM), F32),
                        pltpu.VMEM((SSD_INNER, SSD_STATE), F32),
                        pltpu.VMEM((q, SSD_INNER), F32)],
        compiler_params=_params("parallel", "arbitrary"),
        name="ssd_mixer",
    )(z, xbc, dt, buf, h0, cw, cb, alog, dlane, gn, sel)


def _cc_kernel(u_ref, buf_ref, w_ref, b_ref, g_ref, bb_ref, y_ref, nbuf_ref, ext_ref, *, tt, nt):
    j = pl.program_id(1)
    hist = CC_WIDTH - 1
    pad = 32
    rb = min(tt, 32)

    @pl.when(j == 0)
    def _():
        ext_ref[pl.ds(pad - hist, hist), :] = buf_ref[0]

    ext_ref[pl.ds(pad, tt), :] = u_ref[...]
    for r0 in range(0, tt, rb):
        acc = jnp.zeros((rb, CC_CH), F32) + b_ref[...]
        for k in range(CC_WIDTH):
            acc = acc + ext_ref[pl.ds(pad - hist + k + r0, rb), :] * w_ref[pl.ds(k, 1), :]
        mu = jnp.mean(acc, axis=-1, keepdims=True)
        xc = acc - mu
        v = xc * lax.rsqrt(jnp.mean(xc * xc, axis=-1, keepdims=True) + EPS) * g_ref[...] + bb_ref[...]
        y_ref[pl.ds(r0, rb), :] = _silu(v).astype(BF16)

    @pl.when(j == nt - 1)
    def _():
        nbuf_ref[0] = ext_ref[pl.ds(pad + tt - hist, hist), :]

    if nt > 1:
        ext_ref[pl.ds(pad - hist, hist), :] = ext_ref[pl.ds(pad + tt - hist, hist), :]


def cc_mixer(u, buf, w, b, g, bb, *, nb, seq):
    tt = min(seq, 128)
    nt = seq // tt
    assert nt == 1 or tt >= CC_WIDTH - 1
    return pl.pallas_call(
        functools.partial(_cc_kernel, tt=tt, nt=nt),
        out_shape=(jax.ShapeDtypeStruct((nb * seq, CC_CH), BF16),
                   jax.ShapeDtypeStruct((nb, CC_WIDTH - 1, CC_CH), F32)),
        grid=(nb, nt),
        in_specs=[pl.BlockSpec((tt, CC_CH), lambda i, j: (i * nt + j, 0)),
                  pl.BlockSpec((1, CC_WIDTH - 1, CC_CH), lambda i, j: (i, 0, 0)),
                  _resident(w.shape), _resident(b.shape), _resident(g.shape), _resident(bb.shape)],
        out_specs=(pl.BlockSpec((tt, CC_CH), lambda i, j: (i * nt + j, 0)),
                   pl.BlockSpec((1, CC_WIDTH - 1, CC_CH), lambda i, j: (i, 0, 0))),
        scratch_shapes=[pltpu.VMEM((32 + tt, CC_CH), F32)],
        compiler_params=_params("parallel", "arbitrary"),
        name="cc_mixer",
    )(u, buf, w, b, g, bb)


def _fox_prompt_kernel(q_ref, kt_ref, vt_ref, ct_ref, o_ref, m_ref, l_ref, acc_ref, *, tq, tk, nkb):
    hp = pl.program_id(1)
    i = pl.program_id(2)
    q = q_ref[...]
    lane = lax.broadcasted_iota(jnp.int32, (tq, LANES), 1)
    row_g = i * tq + lax.broadcasted_iota(jnp.int32, (tq, tk), 0)
    col_l = lax.broadcasted_iota(jnp.int32, (tq, tk), 1)
    m_ref[...] = jnp.full(m_ref.shape, -1e30, F32)
    l_ref[...] = jnp.zeros(l_ref.shape, F32)
    acc_ref[...] = jnp.zeros(acc_ref.shape, F32)
    reps = tk // LANES
    for kb in range(nkb):
        @pl.when(kb * tk < (i + 1) * tq)
        def _():
            k16 = kt_ref[0, :, kb * tk:(kb + 1) * tk].astype(BF16)
            v16 = vt_ref[0, :, kb * tk:(kb + 1) * tk].astype(BF16)
            keep = (kb * tk + col_l) <= row_g
            for hh in range(2):
                qh = jnp.where((lane >= hh * FOX_HEAD_DIM) & (lane < (hh + 1) * FOX_HEAD_DIM), q, jnp.zeros_like(q))
                s = _dot(qh, k16)
                s = s - ct_ref[0, pl.ds(2 * hp + hh, 1), kb * tk:(kb + 1) * tk]
                s = jnp.where(keep, s, -jnp.inf)
                m_prev = m_ref[hh]
                m_new = jnp.maximum(m_prev, jnp.max(s, axis=-1, keepdims=True))
                alpha = jnp.exp(m_prev - m_new)
                p = jnp.exp(s - jnp.tile(m_new, (1, reps)))
                l_ref[hh] = alpha * l_ref[hh] + jnp.sum(p, axis=-1, keepdims=True)
                acc_ref[hh] = alpha * acc_ref[hh] + _dot_nt(p.astype(BF16), v16)
                m_ref[hh] = m_new
    o0 = acc_ref[0] / l_ref[0]
    o1 = acc_ref[1] / l_ref[1]
    o_ref[...] = jnp.where(lane < FOX_HEAD_DIM, o0, o1).astype(BF16)


def fox_prompt(q, kt, vt, ct, *, nb, seq):
    tq = tk = min(seq, 256)
    nq = seq // tq
    return pl.pallas_call(
        functools.partial(_fox_prompt_kernel, tq=tq, tk=tk, nkb=seq // tk),
        out_shape=jax.ShapeDtypeStruct((nb * seq, FOX_WIDTH), BF16),
        grid=(nb, FOX_HEADS // 2, nq),
        in_specs=[pl.BlockSpec((tq, LANES), lambda b, h, i: (b * nq + i, h)),
                  pl.BlockSpec((1, LANES, seq), lambda b, h, i: (b, h, 0)),
                  pl.BlockSpec((1, LANES, seq), lambda b, h, i: (b, h, 0)),
                  pl.BlockSpec((1, HEAD_ROWS, seq), lambda b, h, i: (b, 0, 0))],
        out_specs=pl.BlockSpec((tq, LANES), lambda b, h, i: (b * nq + i, h)),
        scratch_shapes=[pltpu.VMEM((2, tq, LANES), F32), pltpu.VMEM((2, tq, LANES), F32),
                        pltpu.VMEM((2, tq, LANES), F32)],
        compiler_params=_params("parallel", "parallel", "arbitrary"),
        name="fox_prompt",
    )(q, kt, vt, ct)


def _fox_sample_kernel(pt_ref, qbd_ref, knew_ref, vnew_ref, lfnew_ref, *refs, layer, npg, nsteps, tnew):
    del pt_ref, layer
    k_refs = refs[:npg]
    v_refs = refs[npg:2 * npg]
    f_refs = refs[2 * npg:3 * npg]
    o_ref = refs[3 * npg]
    m_ref, l_ref, acc_ref, carry_ref = refs[3 * npg + 1:]
    j = pl.program_id(1)
    rows = 8 * HEAD_ROWS

    @pl.when(j == 0)
    def _():
        m_ref[...] = jnp.full(m_ref.shape, -1e30, F32)
        l_ref[...] = jnp.zeros(l_ref.shape, F32)
        acc_ref[...] = jnp.zeros(acc_ref.shape, F32)
        carry_ref[...] = jnp.zeros(carry_ref.shape, F32)

    qbd = qbd_ref[0]
    lane = lax.broadcasted_iota(jnp.int32, (HEAD_ROWS, LANES), 1)

    def prefix(lf):
        c = lf
        sh = 1
        while sh < LANES:
            c = c + jnp.where(lane >= sh, pltpu.roll(c, sh, axis=1), 0.0)
            sh *= 2
        return c

    def attend(kt, vt, lf, keep):
        cum = prefix(lf) + carry_ref[...]
        carry_ref[...] = carry_ref[...] + jnp.sum(lf, axis=-1, keepdims=True)
        s = _dot(qbd, kt.astype(BF16)) - jnp.tile(cum, (rows // HEAD_ROWS, 1))
        if keep is not None:
            s = jnp.where(keep, s, -jnp.inf)
        m_prev = m_ref[...]
        m_new = jnp.maximum(m_prev, jnp.max(s, axis=-1, keepdims=True))
        alpha = jnp.exp(m_prev - m_new)
        p = jnp.exp(s - m_new)
        l_ref[...] = alpha * l_ref[...] + jnp.sum(p, axis=-1, keepdims=True)
        m_ref[...] = m_new
        pv = _dot_nt(vt.astype(BF16), p.astype(BF16))
        acc_ref[...] = acc_ref[...] * alpha.T[0:1, :] + pv

    zpad = jnp.zeros((HEAD_ROWS - FOX_HEADS, LANES), F32)
    for r in range(npg):
        lf = jnp.concatenate([f_refs[r][0], zpad], axis=0)
        attend(k_refs[r][0, 0], v_refs[r][0, 0], lf, None)

    @pl.when(j == nsteps - 1)
    def _():
        srow = lax.broadcasted_iota(jnp.int32, (rows, LANES), 0)
        scol = lax.broadcasted_iota(jnp.int32, (rows, LANES), 1)
        keep = (scol <= srow // HEAD_ROWS) & (scol < tnew)
        lf_new = jnp.where(lane < tnew, lfnew_ref[0], 0.0)
        attend(knew_ref[0], vnew_ref[0], lf_new, keep)
        o_t = acc_ref[...] * (1.0 / l_ref[...]).T[0:1, :]
        o = o_t.T
        orow = lax.broadcasted_iota(jnp.int32, (rows, FOX_WIDTH), 0)
        ocol = lax.broadcasted_iota(jnp.int32, (rows, FOX_WIDTH), 1)
        o = jnp.where((orow % HEAD_ROWS) == (ocol // FOX_HEAD_DIM), o, 0.0)
        pick = (lax.broadcasted_iota(jnp.int32, (8, rows), 1) // HEAD_ROWS
                == lax.broadcasted_iota(jnp.int32, (8, rows), 0)).astype(F32)
        o_ref[0] = _dot_hi(pick, o).astype(BF16)


def fox_sample(qbd, knew, vnew, lfnew, kcache, vcache, fcache, page_table, *, layer, tnew):
    nb, n_pages = page_table.shape
    npg = min(PAGES_PER_STEP, n_pages)
    nsteps = n_pages // npg
    assert tnew <= 8

    def page_spec(r, shape, kind):
        if kind == "kv":
            return pl.BlockSpec((1, 1, FOX_WIDTH, PAGE_SIZE), lambda b, j, pt: (layer, pt[b, j * npg + r], 0, 0))
        return pl.BlockSpec((1, FOX_HEADS, PAGE_SIZE), lambda b, j, pt: (layer, 0, pt[b, j * npg + r]))

    per_seq = lambda shape: pl.BlockSpec((1,) + shape, lambda b, j, pt: (b, 0, 0))
    in_specs = [per_seq((8 * HEAD_ROWS, FOX_WIDTH)), per_seq((FOX_WIDTH, LANES)), per_seq((FOX_WIDTH, LANES)),
                per_seq((HEAD_ROWS, LANES))]
    in_specs += [page_spec(r, None, "kv") for r in range(npg)]
    in_specs += [page_spec(r, None, "kv") for r in range(npg)]
    in_specs += [page_spec(r, None, "f") for r in range(npg)]
    grid_spec = pltpu.PrefetchScalarGridSpec(
        num_scalar_prefetch=1,
        grid=(nb, nsteps),
        in_specs=in_specs,
        out_specs=pl.BlockSpec((1, 8, FOX_WIDTH), lambda b, j, pt: (b, 0, 0)),
        scratch_shapes=[pltpu.VMEM((8 * HEAD_ROWS, LANES), F32), pltpu.VMEM((8 * HEAD_ROWS, LANES), F32),
                        pltpu.VMEM((FOX_WIDTH, LANES), F32), pltpu.VMEM((HEAD_ROWS, LANES), F32)],
    )
    return pl.pallas_call(
        functools.partial(_fox_sample_kernel, layer=layer, npg=npg, nsteps=nsteps, tnew=tnew),
        out_shape=jax.ShapeDtypeStruct((nb, 8, FOX_WIDTH), BF16),
        grid_spec=grid_spec,
        compiler_params=_params("parallel", "arbitrary"),
        name="fox_sample",
    )(page_table, qbd, knew, vnew, lfnew, *([kcache] * npg), *([vcache] * npg), *([fcache] * npg))


def _outproj_kernel(ys_ref, yc_ref, yf_ref, w_ref, x_ref, g_ref, xo_ref, no_ref):
    a = SSD_INNER
    b = SSD_INNER + CC_CH
    acc = _dot(ys_ref[...], w_ref[0:a, :])
    acc = acc + _dot(yc_ref[...], w_ref[a:b, :])
    acc = acc + _dot(yf_ref[...], w_ref[b:, :])
    xn = x_ref[...] + acc
    xo_ref[...] = xn
    no_ref[...] = _rms_rows(xn, g_ref[...]).astype(BF16)


def out_proj(ys, yc, yf, w, x, g):
    m, d = x.shape
    tm = min(m, 512)
    row = lambda wd: pl.BlockSpec((tm, wd), lambda i: (i, 0))
    return pl.pallas_call(
        _outproj_kernel,
        out_shape=(jax.ShapeDtypeStruct((m, d), F32), jax.ShapeDtypeStruct((m, d), BF16)),
        grid=(m // tm,),
        in_specs=[row(SSD_INNER), row(CC_CH), row(FOX_WIDTH), _resident(w.shape), row(d), _resident(g.shape)],
        out_specs=(row(d), row(d)),
        compiler_params=_params("parallel"),
        name="out_proj",
    )(ys, yc, yf, w, x, g)


def _memkv_kernel(m_ref, gm_ref, wk_ref, wv_ref, gk_ref, k_ref, v_ref):
    mn = _rms_rows(m_ref[...], gm_ref[...]).astype(BF16)
    k = _dot(mn, wk_ref[...])
    for h in range(XA_HEADS):
        lo, hi = h * XA_HEAD_DIM, (h + 1) * XA_HEAD_DIM
        k_ref[:, lo:hi] = _rms_rows(k[:, lo:hi], gk_ref[...])
    v_ref[...] = _dot(mn, wv_ref[...])


def mem_kv(mem, gm, wk, wv, gk):
    m, d = mem.shape
    tm = min(m, 256)
    row = lambda wd: pl.BlockSpec((tm, wd), lambda i: (i, 0))
    return pl.pallas_call(
        _memkv_kernel,
        out_shape=(jax.ShapeDtypeStruct((m, XA_WIDTH), F32), jax.ShapeDtypeStruct((m, XA_WIDTH), F32)),
        grid=(m // tm,),
        in_specs=[row(d), _resident(gm.shape), _resident(wk.shape), _resident(wv.shape), _resident(gk.shape)],
        out_specs=(row(XA_WIDTH), row(XA_WIDTH)),
        compiler_params=_params("parallel"),
        name="mem_kv",
    )(mem, gm, wk, wv, gk)


def _xattn_kernel(n_ref, x_ref, wq_ref, qg_ref, mk_ref, mv_ref, wo_ref, g_ref, xo_ref, no_ref):
    q = _dot(n_ref[...], wq_ref[...])
    mk = mk_ref[0].astype(BF16)
    mv = mv_ref[0].astype(BF16)
    outs = []
    for h in range(XA_HEADS):
        lo, hi = h * XA_HEAD_DIM, (h + 1) * XA_HEAD_DIM
        qh = _rms_rows(q[:, lo:hi], qg_ref[...]).astype(BF16)
        s = _dot_nt(qh, mk[:, lo:hi]) * (XA_HEAD_DIM ** -0.5)
        s = s - jnp.max(s, axis=-1, keepdims=True)
        e = jnp.exp(s)
        p = e / jnp.sum(e, axis=-1, keepdims=True)
        outs.append(_dot(p.astype(BF16), mv[:, lo:hi]))
    o = jnp.concatenate(outs, axis=-1).astype(BF16)
    xn = x_ref[...] + _dot(o, wo_ref[...])
    xo_ref[...] = xn
    no_ref[...] = _rms_rows(xn, g_ref[...]).astype(BF16)


def cross_attn(n, x, wq, qg, mk, mv, wo, g, *, seq):
    m, d = x.shape
    tm = min(seq, 512)
    per = seq // tm
    mem = mk.shape[1]
    row = lambda wd: pl.BlockSpec((tm, wd), lambda i: (i, 0))
    kv = pl.BlockSpec((1, mem, XA_WIDTH), lambda i: (i // per, 0, 0))
    return pl.pallas_call(
        _xattn_kernel,
        out_shape=(jax.ShapeDtypeStruct((m, d), F32), jax.ShapeDtypeStruct((m, d), BF16)),
        grid=(m // tm,),
        in_specs=[row(d), row(d), _resident(wq.shape), _resident(qg.shape), kv, kv, _resident(wo.shape),
                  _resident(g.shape)],
        out_specs=(row(d), row(d)),
        compiler_params=_params("parallel"),
        name="cross_attn",
    )(n, x, wq, qg, mk, mv, wo, g)


def _ffn_kernel(n_ref, x_ref, wg_ref, wu_ref, wd_ref, g_ref, xo_ref, no_ref, acc_ref, *, nf):
    f = pl.program_id(1)

    @pl.when(f == 0)
    def _():
        acc_ref[...] = jnp.zeros_like(acc_ref)

    n = n_ref[...]
    a = (_silu(_dot(n, wg_ref[...])) * _dot(n, wu_ref[...])).astype(BF16)
    acc_ref[...] += _dot(a, wd_ref[...])

    @pl.when(f == nf - 1)
    def _():
        xn = x_ref[...] + acc_ref[...]
        xo_ref[...] = xn
        no_ref[...] = _rms_rows(xn, g_ref[...]).astype(BF16)


def ffn(n, x, wg, wu, wd, g):
    m, d = x.shape
    dff = wg.shape[1]
    tm = min(m, 512)
    tf = 512
    nf = dff // tf
    row = lambda wdt: pl.BlockSpec((tm, wdt), lambda i, f: (i, 0))
    return pl.pallas_call(
        functools.partial(_ffn_kernel, nf=nf),
        out_shape=(jax.ShapeDtypeStruct((m, d), F32), jax.ShapeDtypeStruct((m, d), BF16)),
        grid=(m // tm, nf),
        in_specs=[row(d), row(d),
                  pl.BlockSpec((d, tf), lambda i, f: (0, f)), pl.BlockSpec((d, tf), lambda i, f: (0, f)),
                  pl.BlockSpec((tf, d), lambda i, f: (f, 0)), _resident(g.shape)],
        out_specs=(row(d), row(d)),
        scratch_shapes=[pltpu.VMEM((tm, d), F32)],
        compiler_params=_params("parallel", "arbitrary"),
        name="ffn",
    )(n, x, wg, wu, wd, g)


def _head_selector():
    r = jnp.arange(LANES)[:, None]
    c = jnp.arange(FOX_WIDTH)[None, :]
    return (r == c // FOX_HEAD_DIM).astype(F32)


def _pad_lanes(v, width=LANES):
    return jnp.pad(v, (0, width - v.shape[0]))[None, :]


def kernel(x_prompt, x_sample, mem_prompt, cache_fox_k, cache_fox_v, cache_fox_logf, page_table, state_ssd, state_ssd_conv, state_cc_conv, cache_mem_k, cache_mem_v, norm_mix, w_in, ssd_conv_w, ssd_conv_b, ssd_dt_bias, ssd_a_log, ssd_d, ssd_norm, cc_dw_w, cc_dw_b, cc_ln_g, cc_ln_b, fox_q_norm, fox_k_norm, fox_fg_bias, w_out, norm_xa, norm_mem, xa_wq, xa_wk, xa_wv, xa_q_norm, xa_k_norm, xa_wo, norm_ffn, ffn_wg, ffn_wu, ffn_wd):
    depth = norm_mix.shape[0]
    bp, lp, d = x_prompt.shape
    bs, ls, _ = x_sample.shape
    mp, ms = bp * lp, bs * ls
    n_pool = cache_fox_k.shape[1]
    mem_len = mem_prompt.shape[1]
    q = SSD_CHUNK

    sel = _head_selector()
    selt = sel.T
    kcache = jnp.transpose(cache_fox_k, (0, 1, 3, 4, 2)).reshape(depth, n_pool, FOX_WIDTH, PAGE_SIZE)
    vcache = jnp.transpose(cache_fox_v, (0, 1, 3, 4, 2)).reshape(depth, n_pool, FOX_WIDTH, PAGE_SIZE)
    fcache = jnp.transpose(cache_fox_logf, (0, 3, 1, 2)).reshape(depth, FOX_HEADS, n_pool * PAGE_SIZE)
    w_in_t = jnp.transpose(w_in, (0, 2, 1))

    xp = x_prompt.reshape(mp, d)
    xs = x_sample.reshape(ms, d)
    mem = mem_prompt.reshape(bp * mem_len, d)
    np_ = rmsnorm_bf16(xp, norm_mix[0][None, :])
    ns_ = rmsnorm_bf16(xs, norm_mix[0][None, :])

    zeros_ssd_buf = jnp.zeros((bp, SSD_CONV - 1, SSD_CONV_DIM), F32)
    zeros_ssd_h = jnp.zeros((bp, SSD_INNER, SSD_STATE), F32)
    zeros_cc_buf = jnp.zeros((bp, CC_WIDTH - 1, CC_CH), F32)

    outs = [[] for _ in range(14)]
    for i in range(depth):
        wl = w_in[i]
        wz = wl[:, _OFF_Z:_OFF_XBC].astype(BF16)
        wx = wl[:, _OFF_XBC:_OFF_DT].astype(BF16)
        wdt = jnp.pad(wl[:, _OFF_DT:_OFF_GLU], ((0, 0), (0, LANES - SSD_HEADS))).astype(BF16)
        wglu = wl[:, _OFF_GLU:_OFF_Q].astype(BF16)
        wq = wl[:, _OFF_Q:_OFF_K].astype(BF16)
        wkt = w_in_t[i, _OFF_K:_OFF_V].astype(BF16)
        wvt = w_in_t[i, _OFF_V:_OFF_FG].astype(BF16)
        wft = jnp.pad(w_in_t[i, _OFF_FG:], ((0, HEAD_ROWS - FOX_HEADS), (0, 0))).astype(BF16)
        bd = _pad_lanes(ssd_dt_bias[i])
        qg = (jnp.tile(fox_q_norm[i], FOX_HEADS) * (FOX_HEAD_DIM ** -0.5))[None, :]
        gk = jnp.broadcast_to(jnp.tile(fox_k_norm[i], FOX_HEADS)[:, None], (FOX_WIDTH, LANES))
        bfg = jnp.broadcast_to(jnp.pad(fox_fg_bias[i], (0, HEAD_ROWS - FOX_HEADS))[:, None], (HEAD_ROWS, LANES))
        alog = _pad_lanes(ssd_a_log[i])
        dlane = jnp.repeat(ssd_d[i], SSD_HEAD_DIM)[None, :]
        gn = ssd_norm[i][None, :]
        w_o = w_out[i].astype(BF16)
        wxq = xa_wq[i].astype(BF16)
        wxk = xa_wk[i].astype(BF16)
        wxv = xa_wv[i].astype(BF16)
        wxo = xa_wo[i].astype(BF16)
        wg = ffn_wg[i].astype(BF16)
        wu = ffn_wu[i].astype(BF16)
        wd = ffn_wd[i].astype(BF16)
        g_next = norm_mix[i + 1][None, :] if i + 1 < depth else jnp.ones((1, d), F32)
        conv_args = (ssd_conv_w[i], ssd_conv_b[i][None, :], alog, dlane, gn, sel)
        cc_args = (cc_dw_w[i], cc_dw_b[i][None, :], cc_ln_g[i][None, :], cc_ln_b[i][None, :])

        mk, mv = mem_kv(mem, norm_mem[i][None, :], wxk, wxv, xa_k_norm[i][None, :])
        z, xbc, dt = proj_ssd(np_, wz, wx, wdt, bd)
        ucc, qn = proj_ccq(np_, wglu, wq, qg, sel, selt)
        kt, vt, lft, cumt = proj_t(np_, wkt, wvt, wft, gk, bfg, nb=bp, seq=lp, seg=lp)
        y_ssd, ssd_buf, ssd_h = ssd_mixer(z, xbc, dt, zeros_ssd_buf, zeros_ssd_h, *conv_args,
                                          nb=bp, seq=lp, valid=lp)
        y_cc, cc_buf = cc_mixer(ucc, zeros_cc_buf, *cc_args, nb=bp, seq=lp)
        y_fox = fox_prompt(qn, kt, vt, cumt, nb=bp, seq=lp)
        xp, nxa = out_proj(y_ssd, y_cc, y_fox, w_o, xp, norm_xa[i][None, :])
        xp, nff = cross_attn(nxa, xp, wxq, xa_q_norm[i][None, :], mk.reshape(bp, mem_len, XA_WIDTH),
                             mv.reshape(bp, mem_len, XA_WIDTH), wxo, norm_ffn[i][None, :], seq=lp)
        xp, np_ = ffn(nff, xp, wg, wu, wd, g_next)
        outs[0].append(jnp.transpose(kt.reshape(bp, FOX_HEADS, FOX_HEAD_DIM, lp), (0, 3, 1, 2)))
        outs[1].append(jnp.transpose(vt.reshape(bp, FOX_HEADS, FOX_HEAD_DIM, lp), (0, 3, 1, 2)))
        outs[2].append(jnp.transpose(lft[:, :FOX_HEADS, :], (0, 2, 1)))
        outs[3].append(ssd_h.reshape(bp, SSD_HEADS, SSD_HEAD_DIM, SSD_STATE))
        outs[4].append(ssd_buf)
        outs[5].append(cc_buf)
        outs[6].append(mk.reshape(bp, mem_len, XA_HEADS, XA_HEAD_DIM))
        outs[7].append(mv.reshape(bp, mem_len, XA_HEADS, XA_HEAD_DIM))

        z, xbc, dt = proj_ssd(ns_, wz, wx, wdt, bd)
        ucc, qn = proj_ccq(ns_, wglu, wq, qg, sel, selt)
        kt, vt, lft, _ = proj_t(ns_, wkt, wvt, wft, gk, bfg, nb=1, seq=ms, seg=ls)
        pad_tok = lambda a: jnp.pad(a.reshape(bs, ls, a.shape[-1]), ((0, 0), (0, q - ls), (0, 0))).reshape(bs * q, a.shape[-1])
        y_ssd, ssd_buf, ssd_h = ssd_mixer(pad_tok(z), pad_tok(xbc), pad_tok(dt), state_ssd_conv[i],
                                          state_ssd[i].reshape(bs, SSD_INNER, SSD_STATE), *conv_args,
                                          nb=bs, seq=q, valid=ls)
        y_ssd = y_ssd.reshape(bs, q, SSD_INNER)[:, :ls].reshape(ms, SSD_INNER)
        y_cc, cc_buf = cc_mixer(ucc, state_cc_conv[i], *cc_args, nb=bs, seq=ls)
        q4 = qn.reshape(bs, ls, FOX_HEADS, 1, FOX_HEAD_DIM)
        eye = jnp.eye(FOX_HEADS, dtype=BF16)[None, None, :, :, None]
        qbd = (q4 * eye).reshape(bs, ls, FOX_HEADS, FOX_WIDTH)
        qbd = jnp.pad(qbd, ((0, 0), (0, 8 - ls), (0, HEAD_ROWS - FOX_HEADS), (0, 0))).reshape(bs, 8 * HEAD_ROWS, FOX_WIDTH)
        to_seq = lambda a: jnp.pad(jnp.transpose(a[0].reshape(a.shape[1], bs, ls), (1, 0, 2)),
                                   ((0, 0), (0, 0), (0, LANES - ls)))
        y_fox = fox_sample(qbd, to_seq(kt), to_seq(vt), to_seq(lft), kcache, vcache, fcache, page_table,
                           layer=i, tnew=ls)
        y_fox = y_fox[:, :ls].reshape(ms, FOX_WIDTH)
        xs, nxa = out_proj(y_ssd, y_cc, y_fox, w_o, xs, norm_xa[i][None, :])
        xs, nff = cross_attn(nxa, xs, wxq, xa_q_norm[i][None, :], cache_mem_k[i].reshape(bs, mem_len, XA_WIDTH),
                             cache_mem_v[i].reshape(bs, mem_len, XA_WIDTH), wxo, norm_ffn[i][None, :], seq=ls)
        xs, ns_ = ffn(nff, xs, wg, wu, wd, g_next)
        tok = lambda a, w: jnp.transpose(a[0], (1, 0)).reshape(bs, ls, w)
        outs[8].append(tok(kt, FOX_WIDTH).reshape(bs, ls, FOX_HEADS, FOX_HEAD_DIM))
        outs[9].append(tok(vt, FOX_WIDTH).reshape(bs, ls, FOX_HEADS, FOX_HEAD_DIM))
        outs[10].append(tok(lft, HEAD_ROWS)[:, :, :FOX_HEADS])
        outs[11].append(ssd_h.reshape(bs, SSD_HEADS, SSD_HEAD_DIM, SSD_STATE))
        outs[12].append(ssd_buf)
        outs[13].append(cc_buf)

    return (xp.reshape(bp, lp, d), xs.reshape(bs, ls, d)) + tuple(jnp.stack(o) for o in outs)
```

```python
import functools

import jax
import jax.numpy as jnp
from jax import lax
from jax.experimental import pallas as pl
from jax.experimental.pallas import tpu as pltpu

F32 = jnp.float32
BF16 = jnp.bfloat16
HI = lax.Precision.HIGHEST

D_MODEL = 2048
SSD_HEADS = 12
SSD_HEAD_DIM = 64
SSD_INNER = SSD_HEADS * SSD_HEAD_DIM
SSD_GROUPS = 2
SSD_STATE = 128
SSD_CONV = 4
SSD_CHUNK = 128
SSD_CONV_DIM = SSD_INNER + 2 * SSD_GROUPS * SSD_STATE
CC_CH = 512
CC_WIDTH = 31
FOX_HEADS = 12
FOX_HEAD_DIM = 64
FOX_WIDTH = FOX_HEADS * FOX_HEAD_DIM
PAGE_SIZE = 128
XA_HEADS = 4
XA_HEAD_DIM = 128
XA_WIDTH = XA_HEADS * XA_HEAD_DIM
EPS = 1e-6

LANES = 128
HEAD_ROWS = 16
VMEM_LIMIT_BYTES = 56 * 1024 * 1024
PAGES_PER_STEP = 16

_OFF_Z = 0
_OFF_XBC = _OFF_Z + SSD_INNER
_OFF_DT = _OFF_XBC + SSD_CONV_DIM
_OFF_GLU = _OFF_DT + SSD_HEADS
_OFF_Q = _OFF_GLU + 2 * CC_CH
_OFF_K = _OFF_Q + FOX_WIDTH
_OFF_V = _OFF_K + FOX_WIDTH
_OFF_FG = _OFF_V + FOX_WIDTH


def _params(*sem):
    return pltpu.CompilerParams(dimension_semantics=sem, vmem_limit_bytes=VMEM_LIMIT_BYTES)


def _resident(shape):
    nd = len(shape)
    return pl.BlockSpec(shape, lambda *_: (0,) * nd, pipeline_mode=pl.Buffered(1))


def _rms_rows(x, g):
    ms = jnp.mean(x * x, axis=-1, keepdims=True)
    return x * lax.rsqrt(ms + EPS) * g


def _softplus(x):
    return jnp.maximum(x, 0.0) + jnp.log1p(jnp.exp(-jnp.abs(x)))


def _silu(x):
    return x * jax.nn.sigmoid(x)


def _dot(a, b):
    return jnp.dot(a, b, preferred_element_type=F32)


def _dot_hi(a, b):
    return jnp.dot(a, b, precision=HI, preferred_element_type=F32)


def _dot_split(a, b, passes=3):
    a_exact = a.dtype == BF16
    rest = b if a_exact else a
    out = None
    for _ in range(passes):
        hi = rest.astype(BF16)
        term = _dot(a, hi) if a_exact else _dot(hi, b)
        out = term if out is None else out + term
        rest = rest - hi.astype(F32)
    return out


def _dot_nt(a, b):
    return lax.dot_general(a, b, (((1,), (1,)), ((), ())), preferred_element_type=F32)


def _dot_tn(a, b):
    return lax.dot_general(a, b, (((0,), (0,)), ((), ())), preferred_element_type=F32)


def _norm_kernel(x_ref, g_ref, o_ref):
    o_ref[...] = _rms_rows(x_ref[...], g_ref[...]).astype(BF16)


def rmsnorm_bf16(x, g):
    m, d = x.shape
    tm = min(m, 512)
    return pl.pallas_call(
        _norm_kernel,
        out_shape=jax.ShapeDtypeStruct((m, d), BF16),
        grid=(m // tm,),
        in_specs=[pl.BlockSpec((tm, d), lambda i: (i, 0)), pl.BlockSpec((1, d), lambda i: (0, 0))],
        out_specs=pl.BlockSpec((tm, d), lambda i: (i, 0)),
        compiler_params=_params("parallel"),
        name="rmsnorm_bf16",
    )(x, g)


def _proj_ssd_kernel(n_ref, wz_ref, wx_ref, wd_ref, bd_ref, z_ref, xbc_ref, dt_ref):
    n = n_ref[...]
    z_ref[...] = _dot(n, wz_ref[...])
    xbc_ref[...] = _dot(n, wx_ref[...])
    dt_ref[...] = _softplus(_dot(n, wd_ref[...]) + bd_ref[...])


def proj_ssd(n, wz, wx, wd, bd):
    m, d = n.shape
    tm = min(m, 512)
    row = lambda w: pl.BlockSpec((tm, w), lambda i: (i, 0))
    return pl.pallas_call(
        _proj_ssd_kernel,
        out_shape=(jax.ShapeDtypeStruct((m, SSD_INNER), F32),
                   jax.ShapeDtypeStruct((m, SSD_CONV_DIM), F32),
                   jax.ShapeDtypeStruct((m, LANES), F32)),
        grid=(m // tm,),
        in_specs=[row(d), _resident(wz.shape), _resident(wx.shape), _resident(wd.shape), _resident(bd.shape)],
        out_specs=(row(SSD_INNER), row(SSD_CONV_DIM), row(LANES)),
        compiler_params=_params("parallel"),
        name="proj_ssd",
    )(n, wz, wx, wd, bd)


def _proj_ccq_kernel(n_ref, wg_ref, wq_ref, qg_ref, sel_ref, selt_ref, u_ref, q_ref):
    n = n_ref[...]
    glu = _dot(n, wg_ref[...])
    u_ref[...] = glu[:, :CC_CH] * jax.nn.sigmoid(glu[:, CC_CH:])
    q = _dot(n, wq_ref[...])
    ms = _dot_split(q * q, selt_ref[...], 2) * (1.0 / FOX_HEAD_DIM)
    r = _dot_split(lax.rsqrt(ms + EPS), sel_ref[...], 2)
    q_ref[...] = (q * r * qg_ref[...]).astype(BF16)


def proj_ccq(n, wg, wq, qg, sel, selt):
    m, d = n.shape
    tm = min(m, 512)
    row = lambda w: pl.BlockSpec((tm, w), lambda i: (i, 0))
    return pl.pallas_call(
        _proj_ccq_kernel,
        out_shape=(jax.ShapeDtypeStruct((m, CC_CH), F32), jax.ShapeDtypeStruct((m, FOX_WIDTH), BF16)),
        grid=(m // tm,),
        in_specs=[row(d), _resident(wg.shape), _resident(wq.shape), _resident(qg.shape),
                  _resident(sel.shape), _resident(selt.shape)],
        out_specs=(row(CC_CH), row(FOX_WIDTH)),
        compiler_params=_params("parallel"),
        name="proj_ccq",
    )(n, wg, wq, qg, sel, selt)


def _proj_t_kernel(n_ref, wk_ref, wv_ref, wf_ref, gk_ref, bf_ref, kt_ref, vt_ref, lf_ref, cum_ref, carry_ref,
                   *, tm, seg):
    j = pl.program_id(1)
    n = n_ref[...]
    kt = _dot_nt(wk_ref[...], n)
    k3 = kt.reshape(FOX_HEADS, FOX_HEAD_DIM, tm)
    ms = jnp.mean(k3 * k3, axis=1, keepdims=True)
    k3 = k3 * lax.rsqrt(ms + EPS)
    kt_ref[0] = k3.reshape(FOX_WIDTH, tm) * jnp.tile(gk_ref[...], (1, tm // gk_ref.shape[1]))
    vt_ref[0] = _dot_nt(wv_ref[...], n)
    raw = _dot_nt(wf_ref[...], n) + jnp.tile(bf_ref[...], (1, tm // bf_ref.shape[1]))
    lf = -_softplus(-raw)
    lf_ref[0] = lf

    @pl.when((j * tm) % seg == 0)
    def _():
        carry_ref[...] = jnp.zeros_like(carry_ref)

    s_idx = lax.broadcasted_iota(jnp.int32, (tm, tm), 0)
    t_idx = lax.broadcasted_iota(jnp.int32, (tm, tm), 1)
    upper = s_idx <= t_idx
    if seg % tm != 0:
        upper = upper & ((s_idx // seg) == (t_idx // seg))
    cum = _dot_split(lf, jnp.where(upper, 1.0, 0.0).astype(BF16))
    carry = carry_ref[...]
    cum_ref[0] = cum + jnp.tile(carry, (1, tm // LANES)) if tm >= LANES else cum + carry[:, :tm]
    carry_ref[...] = carry + _dot_split(lf, jnp.ones((tm, LANES), BF16))


def proj_t(n, wk, wv, wf, gk, bf, *, nb, seq, seg):
    m, d = n.shape
    tm = min(seq, 512)
    assert seg % tm == 0 or seq == tm
    nj = seq // tm
    lane = min(tm, LANES)
    gk = gk[:, :lane]
    bf = bf[:, :lane]
    big = lambda rows: pl.BlockSpec((1, rows, tm), lambda b, j: (b, 0, j))
    return pl.pallas_call(
        functools.partial(_proj_t_kernel, tm=tm, seg=seg),
        out_shape=(jax.ShapeDtypeStruct((nb, FOX_WIDTH, seq), F32),
                   jax.ShapeDtypeStruct((nb, FOX_WIDTH, seq), F32),
                   jax.ShapeDtypeStruct((nb, HEAD_ROWS, seq), F32),
                   jax.ShapeDtypeStruct((nb, HEAD_ROWS, seq), F32)),
        grid=(nb, nj),
        in_specs=[pl.BlockSpec((tm, d), lambda b, j: (b * nj + j, 0)),
                  _resident(wk.shape), _resident(wv.shape), _resident(wf.shape),
                  _resident(gk.shape), _resident(bf.shape)],
        out_specs=(big(FOX_WIDTH), big(FOX_WIDTH), big(HEAD_ROWS), big(HEAD_ROWS)),
        scratch_shapes=[pltpu.VMEM((HEAD_ROWS, LANES), F32)],
        compiler_params=_params("parallel", "arbitrary"),
        name="proj_t",
    )(n, wk, wv, wf, gk, bf)


def _ssd_kernel(z_ref, xbc_ref, dt_ref, buf_ref, h0_ref, cw_ref, cb_ref, alog_ref, dl_ref, gn_ref, sel_ref,
                y_ref, nbuf_ref, hl_ref, ext_ref, h_ref, yg_ref, *, nc, valid):
    c = pl.program_id(1)
    q = SSD_CHUNK
    pad = 8
    c_last = (valid - 1) // q
    t_last = (valid - 1) % q

    @pl.when(c == 0)
    def _():
        ext_ref[pl.ds(pad - 3, 3), :] = buf_ref[0]
        h_ref[...] = h0_ref[0]

    ext_ref[pl.ds(pad, q), :] = xbc_ref[...]
    u = jnp.zeros((q, SSD_CONV_DIM), F32) + cb_ref[...]
    for j in range(SSD_CONV):
        u = u + ext_ref[pl.ds(pad - 3 + j, q), :] * cw_ref[pl.ds(j, 1), :]
    u = _silu(u)

    @pl.when(c == c_last)
    def _():
        nbuf_ref[0] = ext_ref[pl.ds(pad + t_last - 2, 3), :]

    ext_ref[pl.ds(pad - 3, 3), :] = ext_ref[pl.ds(pad + q - 3, 3), :]

    row = lax.broadcasted_iota(jnp.int32, (q, LANES), 0)
    col = lax.broadcasted_iota(jnp.int32, (q, LANES), 1)
    dt = jnp.where(c * q + row < valid, dt_ref[...], 0.0)
    a = -jnp.exp(alog_ref[...])
    tril = row >= col
    cum = _dot_split(jnp.where(tril, 1.0, 0.0).astype(BF16), dt * a)
    sel = sel_ref[...]
    cumw = _dot_split(cum, sel)
    dtw = _dot_split(dt, sel)
    cum_t = cum.T
    dt_t = dt.T
    lastw = cumw[q - 1:q, :]
    w_in_state = jnp.exp(lastw - cumw) * dtw
    e_cumw = jnp.exp(cumw)
    e_last = jnp.exp(cum[q - 1:q, :])
    lane_lo = col < SSD_HEAD_DIM
    row_lo = row < SSD_HEAD_DIM

    hpg = SSD_HEADS // SSD_GROUPS
    ssq = jnp.zeros((q, 1), F32)
    for g in range(SSD_GROUPS):
        bm = u[:, SSD_INNER + g * SSD_STATE:SSD_INNER + (g + 1) * SSD_STATE]
        cm = u[:, SSD_INNER + (SSD_GROUPS + g) * SSD_STATE:SSD_INNER + (SSD_GROUPS + g + 1) * SSD_STATE]
        bm16 = bm.astype(BF16)
        cm16 = cm.astype(BF16)
        cb = _dot_nt(cm16, bm16)
        for pp in range(hpg // 2):
            p = g * (hpg // 2) + pp
            lo, hi = p * LANES, (p + 1) * LANES
            xs = u[:, lo:hi]
            xs16 = xs.astype(BF16)
            yd = []
            for hh in range(2):
                h = 2 * p + hh
                seg = cum[:, h:h + 1] - cum_t[h:h + 1, :]
                decay = jnp.exp(jnp.where(tril, seg, -jnp.inf))
                mm = cb * decay * dt_t[h:h + 1, :]
                yd.append(_dot(mm.astype(BF16), xs16))
            y = jnp.where(lane_lo, yd[0], yd[1])
            h_pair = h_ref[pl.ds(lo, LANES), :]
            y = y + _dot_nt(cm16, h_pair.astype(BF16)) * e_cumw[:, lo:hi]
            y = y + dl_ref[:, lo:hi] * xs
            cs = _dot_tn((xs * w_in_state[:, lo:hi]).astype(BF16), bm16)
            dec = jnp.where(row_lo, e_last[:, 2 * p:2 * p + 1], e_last[:, 2 * p + 1:2 * p + 2])
            h_ref[pl.ds(lo, LANES), :] = dec * h_pair + cs
            yg = y * _silu(z_ref[:, lo:hi])
            yg_ref[:, lo:hi] = yg
            ssq = ssq + jnp.sum(yg * yg, axis=-1, keepdims=True)
    scale = lax.rsqrt(ssq * (1.0 / SSD_INNER) + EPS)
    y_ref[...] = (yg_ref[...] * scale * gn_ref[...]).astype(BF16)

    @pl.when(c == nc - 1)
    def _():
        hl_ref[0] = h_ref[...]


def ssd_mixer(z, xbc, dt, buf, h0, cw, cb, alog, dlane, gn, sel, *, nb, seq, valid):
    q = SSD_CHUNK
    nc = seq // q
    row = lambda w: pl.BlockSpec((q, w), lambda b, c: (b * nc + c, 0))
    return pl.pallas_call(
        functools.partial(_ssd_kernel, nc=nc, valid=valid),
        out_shape=(jax.ShapeDtypeStruct((nb * seq, SSD_INNER), BF16),
                   jax.ShapeDtypeStruct((nb, SSD_CONV - 1, SSD_CONV_DIM), F32),
                   jax.ShapeDtypeStruct((nb, SSD_INNER, SSD_STATE), F32)),
        grid=(nb, nc),
        in_specs=[row(SSD_INNER), row(SSD_CONV_DIM), row(LANES),
                  pl.BlockSpec((1, SSD_CONV - 1, SSD_CONV_DIM), lambda b, c: (b, 0, 0)),
                  pl.BlockSpec((1, SSD_INNER, SSD_STATE), lambda b, c: (b, 0, 0)),
                  _resident(cw.shape), _resident(cb.shape), _resident(alog.shape), _resident(dlane.shape),
                  _resident(gn.shape), _resident(sel.shape)],
        out_specs=(row(SSD_INNER),
                   pl.BlockSpec((1, SSD_CONV - 1, SSD_CONV_DIM), lambda b, c: (b, 0, 0)),
                   pl.BlockSpec((1, SSD_INNER, SSD_STATE), lambda b, c: (b, 0, 0))),
        scratch_shapes=[pltpu.VMEM((q + 8, SSD_CONV_DIM), F32),
                        pltpu.VMEM((SSD_INNER, SSD_STATE), F32),
                        pltpu.VMEM((q, SSD_INNER), F32)],
        compiler_params=_params("parallel", "arbitrary"),
        name="ssd_mixer",
    )(z, xbc, dt, buf, h0, cw, cb, alog, dlane, gn, sel)


def _cc_kernel(u_ref, buf_ref, w_ref, b_ref, g_ref, bb_ref, y_ref, nbuf_ref, ext_ref, *, tt, nt):
    j = pl.program_id(1)
    hist = CC_WIDTH - 1
    pad = 32
    rb = min(tt, 32)

    @pl.when(j == 0)
    def _():
        ext_ref[pl.ds(pad - hist, hist), :] = buf_ref[0]

    ext_ref[pl.ds(pad, tt), :] = u_ref[...]
    for r0 in range(0, tt, rb):
        acc = jnp.zeros((rb, CC_CH), F32) + b_ref[...]
        for k in range(CC_WIDTH):
            acc = acc + ext_ref[pl.ds(pad - hist + k + r0, rb), :] * w_ref[pl.ds(k, 1), :]
        mu = jnp.mean(acc, axis=-1, keepdims=True)
        xc = acc - mu
        v = xc * lax.rsqrt(jnp.mean(xc * xc, axis=-1, keepdims=True) + EPS) * g_ref[...] + bb_ref[...]
        y_ref[pl.ds(r0, rb), :] = _silu(v).astype(BF16)

    @pl.when(j == nt - 1)
    def _():
        nbuf_ref[0] = ext_ref[pl.ds(pad + tt - hist, hist), :]

    if nt > 1:
        ext_ref[pl.ds(pad - hist, hist), :] = ext_ref[pl.ds(pad + tt - hist, hist), :]


def cc_mixer(u, buf, w, b, g, bb, *, nb, seq):
    tt = min(seq, 128)
    nt = seq // tt
    assert nt == 1 or tt >= CC_WIDTH - 1
    return pl.pallas_call(
        functools.partial(_cc_kernel, tt=tt, nt=nt),
        out_shape=(jax.ShapeDtypeStruct((nb * seq, CC_CH), BF16),
                   jax.ShapeDtypeStruct((nb, CC_WIDTH - 1, CC_CH), F32)),
        grid=(nb, nt),
        in_specs=[pl.BlockSpec((tt, CC_CH), lambda i, j: (i * nt + j, 0)),
                  pl.BlockSpec((1, CC_WIDTH - 1, CC_CH), lambda i, j: (i, 0, 0)),
                  _resident(w.shape), _resident(b.shape), _resident(g.shape), _resident(bb.shape)],
        out_specs=(pl.BlockSpec((tt, CC_CH), lambda i, j: (i * nt + j, 0)),
                   pl.BlockSpec((1, CC_WIDTH - 1, CC_CH), lambda i, j: (i, 0, 0))),
        scratch_shapes=[pltpu.VMEM((32 + tt, CC_CH), F32)],
        compiler_params=_params("parallel", "arbitrary"),
        name="cc_mixer",
    )(u, buf, w, b, g, bb)


def _fox_prompt_kernel(q_ref, kt_ref, vt_ref, ct_ref, o_ref, k16_ref, v16_ref, *, tq, nq):
    hp = pl.program_id(1)
    i = pl.program_id(2)

    @pl.when(i == 0)
    def _():
        k16_ref[...] = kt_ref[0].astype(BF16)
        v16_ref[...] = vt_ref[0].astype(BF16)

    def attend(ii):
        w = (ii + 1) * tq
        q = q_ref[...]
        lane = lax.broadcasted_iota(jnp.int32, (tq, LANES), 1)
        keep = (lax.broadcasted_iota(jnp.int32, (tq, tq), 1) <= lax.broadcasted_iota(jnp.int32, (tq, tq), 0))
        k16 = k16_ref[:, :w]
        v16 = v16_ref[:, :w]
        outs = []
        for hh in range(2):
            qh = jnp.where((lane >= hh * FOX_HEAD_DIM) & (lane < (hh + 1) * FOX_HEAD_DIM), q, jnp.zeros_like(q))
            s = _dot(qh, k16) - ct_ref[0, pl.ds(2 * hp + hh, 1), :w]
            tail = jnp.where(keep, s[:, w - tq:], -jnp.inf)
            s = tail if ii == 0 else jnp.concatenate([s[:, :w - tq], tail], axis=1)
            p = jnp.exp(s - jnp.max(s, axis=-1, keepdims=True))
            l = jnp.sum(p, axis=-1, keepdims=True)
            outs.append(_dot_nt(p.astype(BF16), v16) / l)
        o_ref[...] = jnp.where(lane < FOX_HEAD_DIM, outs[0], outs[1]).astype(BF16)

    for ii in range(nq):
        pl.when(i == ii)(functools.partial(attend, ii))


def fox_prompt(q, kt, vt, ct, *, nb, seq):
    tq = min(seq, 256)
    nq = seq // tq
    return pl.pallas_call(
        functools.partial(_fox_prompt_kernel, tq=tq, nq=nq),
        out_shape=jax.ShapeDtypeStruct((nb * seq, FOX_WIDTH), BF16),
        grid=(nb, FOX_HEADS // 2, nq),
        in_specs=[pl.BlockSpec((tq, LANES), lambda b, h, i: (b * nq + i, h)),
                  pl.BlockSpec((1, LANES, seq), lambda b, h, i: (b, h, 0)),
                  pl.BlockSpec((1, LANES, seq), lambda b, h, i: (b, h, 0)),
                  pl.BlockSpec((1, HEAD_ROWS, seq), lambda b, h, i: (b, 0, 0))],
        out_specs=pl.BlockSpec((tq, LANES), lambda b, h, i: (b * nq + i, h)),
        scratch_shapes=[pltpu.VMEM((LANES, seq), BF16), pltpu.VMEM((LANES, seq), BF16)],
        compiler_params=_params("parallel", "parallel", "arbitrary"),
        name="fox_prompt",
    )(q, kt, vt, ct)


def _fox_sample_kernel(pt_ref, qbd_ref, knew_ref, vnew_ref, lfnew_ref, *refs, layer, npg, nsteps, tnew):
    del pt_ref, layer
    k_refs = refs[:npg]
    v_refs = refs[npg:2 * npg]
    f_refs = refs[2 * npg:3 * npg]
    o_ref = refs[3 * npg]
    m_ref, l_ref, acc_ref, carry_ref, kcat_ref, vcat_ref = refs[3 * npg + 1:]
    j = pl.program_id(1)
    rows = 8 * HEAD_ROWS

    @pl.when(j == 0)
    def _():
        m_ref[...] = jnp.full(m_ref.shape, -1e30, F32)
        l_ref[...] = jnp.zeros(l_ref.shape, F32)
        acc_ref[...] = jnp.zeros(acc_ref.shape, F32)
        carry_ref[...] = jnp.zeros(carry_ref.shape, F32)

    qbd = qbd_ref[0]
    lane = lax.broadcasted_iota(jnp.int32, (HEAD_ROWS, LANES), 1)

    def prefix(lf):
        c = lf
        sh = 1
        while sh < LANES:
            c = c + jnp.where(lane >= sh, pltpu.roll(c, sh, axis=1), 0.0)
            sh *= 2
        return c

    def attend(kt16, vt16, lfs, keep):
        s_all = _dot(qbd, kt16)
        carry = carry_ref[...]
        ss = []
        for r, lf in enumerate(lfs):
            cum = prefix(lf) + carry
            carry = carry + jnp.sum(lf, axis=-1, keepdims=True)
            s = s_all[:, r * LANES:(r + 1) * LANES] - jnp.tile(cum, (rows // HEAD_ROWS, 1))
            if keep is not None:
                s = jnp.where(keep, s, -jnp.inf)
            ss.append(s)
        carry_ref[...] = carry
        m_cur = ss[0]
        for s in ss[1:]:
            m_cur = jnp.maximum(m_cur, s)
        m_prev = m_ref[...]
        m_new = jnp.maximum(m_prev, jnp.max(m_cur, axis=-1, keepdims=True))
        alpha = jnp.exp(m_prev - m_new)
        ps = [jnp.exp(s - m_new) for s in ss]
        p_sum = ps[0]
        for p in ps[1:]:
            p_sum = p_sum + p
        l_ref[...] = alpha * l_ref[...] + jnp.sum(p_sum, axis=-1, keepdims=True)
        m_ref[...] = m_new
        p16 = jnp.concatenate([p.astype(BF16) for p in ps], axis=1) if len(ps) > 1 else ps[0].astype(BF16)
        pv = _dot_nt(vt16, p16)
        acc_ref[...] = acc_ref[...] * alpha.T[0:1, :] + pv

    zpad = jnp.zeros((HEAD_ROWS - FOX_HEADS, LANES), F32)
    for r in range(npg):
        kcat_ref[:, r * LANES:(r + 1) * LANES] = k_refs[r][0, 0].astype(BF16)
        vcat_ref[:, r * LANES:(r + 1) * LANES] = v_refs[r][0, 0].astype(BF16)
    attend(kcat_ref[...], vcat_ref[...],
           [jnp.concatenate([f_refs[r][0], zpad], axis=0) for r in range(npg)], None)

    @pl.when(j == nsteps - 1)
    def _():
        srow = lax.broadcasted_iota(jnp.int32, (rows, LANES), 0)
        scol = lax.broadcasted_iota(jnp.int32, (rows, LANES), 1)
        keep = (scol <= srow // HEAD_ROWS) & (scol < tnew)
        lf_new = jnp.where(lane < tnew, lfnew_ref[0], 0.0)
        attend(knew_ref[0].astype(BF16), vnew_ref[0].astype(BF16), [lf_new], keep)
        o_t = acc_ref[...] * (1.0 / l_ref[...]).T[0:1, :]
        o = o_t.T
        orow = lax.broadcasted_iota(jnp.int32, (rows, FOX_WIDTH), 0)
        ocol = lax.broadcasted_iota(jnp.int32, (rows, FOX_WIDTH), 1)
        o = jnp.where((orow % HEAD_ROWS) == (ocol // FOX_HEAD_DIM), o, 0.0)
        pick = (lax.broadcasted_iota(jnp.int32, (8, rows), 1) // HEAD_ROWS
                == lax.broadcasted_iota(jnp.int32, (8, rows), 0)).astype(F32)
        o_ref[0] = _dot_hi(pick, o).astype(BF16)


def fox_sample(qbd, knew, vnew, lfnew, kcache, vcache, fcache, page_table, *, layer, tnew):
    nb, n_pages = page_table.shape
    npg = min(PAGES_PER_STEP, n_pages)
    nsteps = n_pages // npg
    assert tnew <= 8

    def page_spec(r, shape, kind):
        if kind == "kv":
            return pl.BlockSpec((1, 1, FOX_WIDTH, PAGE_SIZE), lambda b, j, pt: (layer, pt[b, j * npg + r], 0, 0))
        return pl.BlockSpec((1, FOX_HEADS, PAGE_SIZE), lambda b, j, pt: (layer, 0, pt[b, j * npg + r]))

    per_seq = lambda shape: pl.BlockSpec((1,) + shape, lambda b, j, pt: (b, 0, 0))
    in_specs = [per_seq((8 * HEAD_ROWS, FOX_WIDTH)), per_seq((FOX_WIDTH, LANES)), per_seq((FOX_WIDTH, LANES)),
                per_seq((HEAD_ROWS, LANES))]
    in_specs += [page_spec(r, None, "kv") for r in range(npg)]
    in_specs += [page_spec(r, None, "kv") for r in range(npg)]
    in_specs += [page_spec(r, None, "f") for r in range(npg)]
    grid_spec = pltpu.PrefetchScalarGridSpec(
        num_scalar_prefetch=1,
        grid=(nb, nsteps),
        in_specs=in_specs,
        out_specs=pl.BlockSpec((1, 8, FOX_WIDTH), lambda b, j, pt: (b, 0, 0)),
        scratch_shapes=[pltpu.VMEM((8 * HEAD_ROWS, LANES), F32), pltpu.VMEM((8 * HEAD_ROWS, LANES), F32),
                        pltpu.VMEM((FOX_WIDTH, LANES), F32), pltpu.VMEM((HEAD_ROWS, LANES), F32),
                        pltpu.VMEM((FOX_WIDTH, npg * PAGE_SIZE), BF16),
                        pltpu.VMEM((FOX_WIDTH, npg * PAGE_SIZE), BF16)],
    )
    return pl.pallas_call(
        functools.partial(_fox_sample_kernel, layer=layer, npg=npg, nsteps=nsteps, tnew=tnew),
        out_shape=jax.ShapeDtypeStruct((nb, 8, FOX_WIDTH), BF16),
        grid_spec=grid_spec,
        compiler_params=_params("parallel", "arbitrary"),
        name="fox_sample",
    )(page_table, qbd, knew, vnew, lfnew, *([kcache] * npg), *([vcache] * npg), *([fcache] * npg))


def _outproj_kernel(ys_ref, yc_ref, yf_ref, w_ref, x_ref, g_ref, xo_ref, no_ref):
    a = SSD_INNER
    b = SSD_INNER + CC_CH
    acc = _dot(ys_ref[...], w_ref[0:a, :])
    acc = acc + _dot(yc_ref[...], w_ref[a:b, :])
    acc = acc + _dot(yf_ref[...], w_ref[b:, :])
    xn = x_ref[...] + acc
    xo_ref[...] = xn
    no_ref[...] = _rms_rows(xn, g_ref[...]).astype(BF16)


def out_proj(ys, yc, yf, w, x, g):
    m, d = x.shape
    tm = min(m, 512)
    row = lambda wd: pl.BlockSpec((tm, wd), lambda i: (i, 0))
    return pl.pallas_call(
        _outproj_kernel,
        out_shape=(jax.ShapeDtypeStruct((m, d), F32), jax.ShapeDtypeStruct((m, d), BF16)),
        grid=(m // tm,),
        in_specs=[row(SSD_INNER), row(CC_CH), row(FOX_WIDTH), _resident(w.shape), row(d), _resident(g.shape)],
        out_specs=(row(d), row(d)),
        compiler_params=_params("parallel"),
        name="out_proj",
    )(ys, yc, yf, w, x, g)


def _memkv_kernel(m_ref, gm_ref, wk_ref, wv_ref, gk_ref, k_ref, v_ref):
    mn = _rms_rows(m_ref[...], gm_ref[...]).astype(BF16)
    k = _dot(mn, wk_ref[...])
    for h in range(XA_HEADS):
        lo, hi = h * XA_HEAD_DIM, (h + 1) * XA_HEAD_DIM
        k_ref[:, lo:hi] = _rms_rows(k[:, lo:hi], gk_ref[...])
    v_ref[...] = _dot(mn, wv_ref[...])


def mem_kv(mem, gm, wk, wv, gk):
    m, d = mem.shape
    tm = min(m, 256)
    row = lambda wd: pl.BlockSpec((tm, wd), lambda i: (i, 0))
    return pl.pallas_call(
        _memkv_kernel,
        out_shape=(jax.ShapeDtypeStruct((m, XA_WIDTH), F32), jax.ShapeDtypeStruct((m, XA_WIDTH), F32)),
        grid=(m // tm,),
        in_specs=[row(d), _resident(gm.shape), _resident(wk.shape), _resident(wv.shape), _resident(gk.shape)],
        out_specs=(row(XA_WIDTH), row(XA_WIDTH)),
        compiler_params=_params("parallel"),
        name="mem_kv",
    )(mem, gm, wk, wv, gk)


def _xattn_kernel(n_ref, x_ref, wq_ref, qg_ref, mk_ref, mv_ref, wo_ref, g_ref, xo_ref, no_ref):
    q = _dot(n_ref[...], wq_ref[...])
    mk = mk_ref[0].astype(BF16)
    mv = mv_ref[0].astype(BF16)
    outs = []
    for h in range(XA_HEADS):
        lo, hi = h * XA_HEAD_DIM, (h + 1) * XA_HEAD_DIM
        qh = _rms_rows(q[:, lo:hi], qg_ref[...]).astype(BF16)
        s = _dot_nt(qh, mk[:, lo:hi]) * (XA_HEAD_DIM ** -0.5)
        s = s - jnp.max(s, axis=-1, keepdims=True)
        e = jnp.exp(s)
        p = e / jnp.sum(e, axis=-1, keepdims=True)
        outs.append(_dot(p.astype(BF16), mv[:, lo:hi]))
    o = jnp.concatenate(outs, axis=-1).astype(BF16)
    xn = x_ref[...] + _dot(o, wo_ref[...])
    xo_ref[...] = xn
    no_ref[...] = _rms_rows(xn, g_ref[...]).astype(BF16)


def cross_attn(n, x, wq, qg, mk, mv, wo, g, *, seq):
    m, d = x.shape
    tm = min(seq, 512)
    per = seq // tm
    mem = mk.shape[1]
    row = lambda wd: pl.BlockSpec((tm, wd), lambda i: (i, 0))
    kv = pl.BlockSpec((1, mem, XA_WIDTH), lambda i: (i // per, 0, 0))
    return pl.pallas_call(
        _xattn_kernel,
        out_shape=(jax.ShapeDtypeStruct((m, d), F32), jax.ShapeDtypeStruct((m, d), BF16)),
        grid=(m // tm,),
        in_specs=[row(d), row(d), _resident(wq.shape), _resident(qg.shape), kv, kv, _resident(wo.shape),
                  _resident(g.shape)],
        out_specs=(row(d), row(d)),
        compiler_params=_params("parallel"),
        name="cross_attn",
    )(n, x, wq, qg, mk, mv, wo, g)


def _ffn_kernel(n_ref, x_ref, wg_ref, wu_ref, wd_ref, g_ref, xo_ref, no_ref, *, nf):
    f = pl.program_id(1)

    @pl.when(f == 0)
    def _():
        xo_ref[...] = x_ref[...]

    n = n_ref[...]
    a = (_silu(_dot(n, wg_ref[...])) * _dot(n, wu_ref[...])).astype(BF16)
    xo_ref[...] += _dot(a, wd_ref[...])

    @pl.when(f == nf - 1)
    def _():
        no_ref[...] = _rms_rows(xo_ref[...], g_ref[...]).astype(BF16)


def ffn(n, x, wg, wu, wd, g):
    m, d = x.shape
    dff = wg.shape[1]
    tm = min(m, 1024)
    tf = 512 if tm < 1024 else 256
    nf = dff // tf
    row = lambda wdt: pl.BlockSpec((tm, wdt), lambda i, f: (i, 0))
    x_spec = pl.BlockSpec((tm, d), lambda i, f: (i, 0), pipeline_mode=pl.Buffered(1))
    return pl.pallas_call(
        functools.partial(_ffn_kernel, nf=nf),
        out_shape=(jax.ShapeDtypeStruct((m, d), F32), jax.ShapeDtypeStruct((m, d), BF16)),
        grid=(m // tm, nf),
        in_specs=[row(d), x_spec,
                  pl.BlockSpec((d, tf), lambda i, f: (0, f)), pl.BlockSpec((d, tf), lambda i, f: (0, f)),
                  pl.BlockSpec((tf, d), lambda i, f: (f, 0)), _resident(g.shape)],
        out_specs=(row(d), row(d)),
        compiler_params=_params("parallel", "arbitrary"),
        name="ffn",
    )(n, x, wg, wu, wd, g)


def _head_selector():
    r = jnp.arange(LANES)[:, None]
    c = jnp.arange(FOX_WIDTH)[None, :]
    return (r == c // FOX_HEAD_DIM).astype(BF16)


def _pad_lanes(v, width=LANES):
    return jnp.pad(v, (0, width - v.shape[0]))[None, :]


def kernel(x_prompt, x_sample, mem_prompt, cache_fox_k, cache_fox_v, cache_fox_logf, page_table, state_ssd, state_ssd_conv, state_cc_conv, cache_mem_k, cache_mem_v, norm_mix, w_in, ssd_conv_w, ssd_conv_b, ssd_dt_bias, ssd_a_log, ssd_d, ssd_norm, cc_dw_w, cc_dw_b, cc_ln_g, cc_ln_b, fox_q_norm, fox_k_norm, fox_fg_bias, w_out, norm_xa, norm_mem, xa_wq, xa_wk, xa_wv, xa_q_norm, xa_k_norm, xa_wo, norm_ffn, ffn_wg, ffn_wu, ffn_wd):
    depth = norm_mix.shape[0]
    bp, lp, d = x_prompt.shape
    bs, ls, _ = x_sample.shape
    mp, ms = bp * lp, bs * ls
    n_pool = cache_fox_k.shape[1]
    mem_len = mem_prompt.shape[1]
    q = SSD_CHUNK

    sel = _head_selector()
    selt = sel.T
    kcache = jnp.transpose(cache_fox_k, (0, 1, 3, 4, 2)).reshape(depth, n_pool, FOX_WIDTH, PAGE_SIZE)
    vcache = jnp.transpose(cache_fox_v, (0, 1, 3, 4, 2)).reshape(depth, n_pool, FOX_WIDTH, PAGE_SIZE)
    fcache = jnp.transpose(cache_fox_logf, (0, 3, 1, 2)).reshape(depth, FOX_HEADS, n_pool * PAGE_SIZE)
    w_in_t = jnp.transpose(w_in, (0, 2, 1))

    xp = x_prompt.reshape(mp, d)
    xs = x_sample.reshape(ms, d)
    mem = mem_prompt.reshape(bp * mem_len, d)
    np_ = rmsnorm_bf16(xp, norm_mix[0][None, :])
    ns_ = rmsnorm_bf16(xs, norm_mix[0][None, :])

    zeros_ssd_buf = jnp.zeros((bp, SSD_CONV - 1, SSD_CONV_DIM), F32)
    zeros_ssd_h = jnp.zeros((bp, SSD_INNER, SSD_STATE), F32)
    zeros_cc_buf = jnp.zeros((bp, CC_WIDTH - 1, CC_CH), F32)

    outs = [[] for _ in range(14)]
    for i in range(depth):
        wl = w_in[i]
        wz = wl[:, _OFF_Z:_OFF_XBC].astype(BF16)
        wx = wl[:, _OFF_XBC:_OFF_DT].astype(BF16)
        wdt = jnp.pad(wl[:, _OFF_DT:_OFF_GLU], ((0, 0), (0, LANES - SSD_HEADS))).astype(BF16)
        wglu = wl[:, _OFF_GLU:_OFF_Q].astype(BF16)
        wq = wl[:, _OFF_Q:_OFF_K].astype(BF16)
        wkt = w_in_t[i, _OFF_K:_OFF_V].astype(BF16)
        wvt = w_in_t[i, _OFF_V:_OFF_FG].astype(BF16)
        wft = jnp.pad(w_in_t[i, _OFF_FG:], ((0, HEAD_ROWS - FOX_HEADS), (0, 0))).astype(BF16)
        bd = _pad_lanes(ssd_dt_bias[i])
        qg = (jnp.tile(fox_q_norm[i], FOX_HEADS) * (FOX_HEAD_DIM ** -0.5))[None, :]
        gk = jnp.broadcast_to(jnp.tile(fox_k_norm[i], FOX_HEADS)[:, None], (FOX_WIDTH, LANES))
        bfg = jnp.broadcast_to(jnp.pad(fox_fg_bias[i], (0, HEAD_ROWS - FOX_HEADS))[:, None], (HEAD_ROWS, LANES))
        alog = _pad_lanes(ssd_a_log[i])
        dlane = jnp.repeat(ssd_d[i], SSD_HEAD_DIM)[None, :]
        gn = ssd_norm[i][None, :]
        w_o = w_out[i].astype(BF16)
        wxq = xa_wq[i].astype(BF16)
        wxk = xa_wk[i].astype(BF16)
        wxv = xa_wv[i].astype(BF16)
        wxo = xa_wo[i].astype(BF16)
        wg = ffn_wg[i].astype(BF16)
        wu = ffn_wu[i].astype(BF16)
        wd = ffn_wd[i].astype(BF16)
        g_next = norm_mix[i + 1][None, :] if i + 1 < depth else jnp.ones((1, d), F32)
        conv_args = (ssd_conv_w[i], ssd_conv_b[i][None, :], alog, dlane, gn, sel)
        cc_args = (cc_dw_w[i], cc_dw_b[i][None, :], cc_ln_g[i][None, :], cc_ln_b[i][None, :])

        mk, mv = mem_kv(mem, norm_mem[i][None, :], wxk, wxv, xa_k_norm[i][None, :])
        z, xbc, dt = proj_ssd(np_, wz, wx, wdt, bd)
        ucc, qn = proj_ccq(np_, wglu, wq, qg, sel, selt)
        kt, vt, lft, cumt = proj_t(np_, wkt, wvt, wft, gk, bfg, nb=bp, seq=lp, seg=lp)
        y_ssd, ssd_buf, ssd_h = ssd_mixer(z, xbc, dt, zeros_ssd_buf, zeros_ssd_h, *conv_args,
                                          nb=bp, seq=lp, valid=lp)
        y_cc, cc_buf = cc_mixer(ucc, zeros_cc_buf, *cc_args, nb=bp, seq=lp)
        y_fox = fox_prompt(qn, kt, vt, cumt, nb=bp, seq=lp)
        xp, nxa = out_proj(y_ssd, y_cc, y_fox, w_o, xp, norm_xa[i][None, :])
        xp, nff = cross_attn(nxa, xp, wxq, xa_q_norm[i][None, :], mk.reshape(bp, mem_len, XA_WIDTH),
                             mv.reshape(bp, mem_len, XA_WIDTH), wxo, norm_ffn[i][None, :], seq=lp)
        xp, np_ = ffn(nff, xp, wg, wu, wd, g_next)
        outs[0].append(jnp.transpose(kt.reshape(bp, FOX_HEADS, FOX_HEAD_DIM, lp), (0, 3, 1, 2)))
        outs[1].append(jnp.transpose(vt.reshape(bp, FOX_HEADS, FOX_HEAD_DIM, lp), (0, 3, 1, 2)))
        outs[2].append(jnp.transpose(lft[:, :FOX_HEADS, :], (0, 2, 1)))
        outs[3].append(ssd_h.reshape(bp, SSD_HEADS, SSD_HEAD_DIM, SSD_STATE))
        outs[4].append(ssd_buf)
        outs[5].append(cc_buf)
        outs[6].append(mk.reshape(bp, mem_len, XA_HEADS, XA_HEAD_DIM))
        outs[7].append(mv.reshape(bp, mem_len, XA_HEADS, XA_HEAD_DIM))

        z, xbc, dt = proj_ssd(ns_, wz, wx, wdt, bd)
        ucc, qn = proj_ccq(ns_, wglu, wq, qg, sel, selt)
        kt, vt, lft, _ = proj_t(ns_, wkt, wvt, wft, gk, bfg, nb=1, seq=ms, seg=ls)
        pad_tok = lambda a: jnp.pad(a.reshape(bs, ls, a.shape[-1]), ((0, 0), (0, q - ls), (0, 0))).reshape(bs * q, a.shape[-1])
        y_ssd, ssd_buf, ssd_h = ssd_mixer(pad_tok(z), pad_tok(xbc), pad_tok(dt), state_ssd_conv[i],
                                          state_ssd[i].reshape(bs, SSD_INNER, SSD_STATE), *conv_args,
                                          nb=bs, seq=q, valid=ls)
        y_ssd = y_ssd.reshape(bs, q, SSD_INNER)[:, :ls].reshape(ms, SSD_INNER)
        y_cc, cc_buf = cc_mixer(ucc, state_cc_conv[i], *cc_args, nb=bs, seq=ls)
        q4 = qn.reshape(bs, ls, FOX_HEADS, 1, FOX_HEAD_DIM)
        eye = jnp.eye(FOX_HEADS, dtype=BF16)[None, None, :, :, None]
        qbd = (q4 * eye).reshape(bs, ls, FOX_HEADS, FOX_WIDTH)
        qbd = jnp.pad(qbd, ((0, 0), (0, 8 - ls), (0, HEAD_ROWS - FOX_HEADS), (0, 0))).reshape(bs, 8 * HEAD_ROWS, FOX_WIDTH)
        to_seq = lambda a: jnp.pad(jnp.transpose(a[0].reshape(a.shape[1], bs, ls), (1, 0, 2)),
                                   ((0, 0), (0, 0), (0, LANES - ls)))
        y_fox = fox_sample(qbd, to_seq(kt), to_seq(vt), to_seq(lft), kcache, vcache, fcache, page_table,
                           layer=i, tnew=ls)
        y_fox = y_fox[:, :ls].reshape(ms, FOX_WIDTH)
        xs, nxa = out_proj(y_ssd, y_cc, y_fox, w_o, xs, norm_xa[i][None, :])
        xs, nff = cross_attn(nxa, xs, wxq, xa_q_norm[i][None, :], cache_mem_k[i].reshape(bs, mem_len, XA_WIDTH),
                             cache_mem_v[i].reshape(bs, mem_len, XA_WIDTH), wxo, norm_ffn[i][None, :], seq=ls)
        xs, ns_ = ffn(nff, xs, wg, wu, wd, g_next)
        tok = lambda a, w: jnp.transpose(a[0], (1, 0)).reshape(bs, ls, w)
        outs[8].append(tok(kt, FOX_WIDTH).reshape(bs, ls, FOX_HEADS, FOX_HEAD_DIM))
        outs[9].append(tok(vt, FOX_WIDTH).reshape(bs, ls, FOX_HEADS, FOX_HEAD_DIM))
        outs[10].append(tok(lft, HEAD_ROWS)[:, :, :FOX_HEADS])
        outs[11].append(ssd_h.reshape(bs, SSD_HEADS, SSD_HEAD_DIM, SSD_STATE))
        outs[12].append(ssd_buf)
        outs[13].append(cc_buf)

    return (xp.reshape(bp, lp, d), xs.reshape(bs, ls, d)) + tuple(jnp.stack(o) for o in outs)
```

```python
import functools

import jax
import jax.numpy as jnp
from jax import lax
from jax.experimental import pallas as pl
from jax.experimental.pallas import tpu as pltpu

F32 = jnp.float32
BF16 = jnp.bfloat16
HI = lax.Precision.HIGHEST

D_MODEL = 2048
SSD_HEADS = 12
SSD_HEAD_DIM = 64
SSD_INNER = SSD_HEADS * SSD_HEAD_DIM
SSD_GROUPS = 2
SSD_STATE = 128
SSD_CONV = 4
SSD_CHUNK = 128
SSD_CONV_DIM = SSD_INNER + 2 * SSD_GROUPS * SSD_STATE
CC_CH = 512
CC_WIDTH = 31
FOX_HEADS = 12
FOX_HEAD_DIM = 64
FOX_WIDTH = FOX_HEADS * FOX_HEAD_DIM
PAGE_SIZE = 128
XA_HEADS = 4
XA_HEAD_DIM = 128
XA_WIDTH = XA_HEADS * XA_HEAD_DIM
EPS = 1e-6

LANES = 128
HEAD_ROWS = 16
VMEM_LIMIT_BYTES = 56 * 1024 * 1024
PAGES_PER_STEP = 16

_OFF_Z = 0
_OFF_XBC = _OFF_Z + SSD_INNER
_OFF_DT = _OFF_XBC + SSD_CONV_DIM
_OFF_GLU = _OFF_DT + SSD_HEADS
_OFF_Q = _OFF_GLU + 2 * CC_CH
_OFF_K = _OFF_Q + FOX_WIDTH
_OFF_V = _OFF_K + FOX_WIDTH
_OFF_FG = _OFF_V + FOX_WIDTH

_ROW_Z = 0
_ROW_GLU = SSD_INNER + SSD_CONV_DIM
_ROW_Q = _ROW_GLU + 2 * CC_CH
_ROW_K = _ROW_Q + FOX_WIDTH
_ROW_V = _ROW_K + FOX_WIDTH
_ROW_DT = _ROW_V + FOX_WIDTH
_ROW_FG = _ROW_DT + LANES
_W_ROWS = _ROW_FG + LANES
FFN_TILE = 256


def _params(*sem):
    return pltpu.CompilerParams(dimension_semantics=sem, vmem_limit_bytes=VMEM_LIMIT_BYTES)


def _resident(shape):
    nd = len(shape)
    return pl.BlockSpec(shape, lambda *_: (0,) * nd, pipeline_mode=pl.Buffered(1))


def _layer_block(shape, layer, block=0):
    return pl.BlockSpec((1,) + tuple(shape), lambda *_: (layer, block, 0), pipeline_mode=pl.Buffered(1))


def _rms_rows(x, g):
    ms = jnp.mean(x * x, axis=-1, keepdims=True)
    return x * lax.rsqrt(ms + EPS) * g


def _softplus(x):
    return jnp.maximum(x, 0.0) + jnp.log1p(jnp.exp(-jnp.abs(x)))


def _silu(x):
    return x * jax.nn.sigmoid(x)


def _dot(a, b):
    return jnp.dot(a, b, preferred_element_type=F32)


def _dot_hi(a, b):
    return jnp.dot(a, b, precision=HI, preferred_element_type=F32)


def _dot_split(a, b, passes=3):
    a_exact = a.dtype == BF16
    rest = b if a_exact else a
    out = None
    for _ in range(passes):
        hi = rest.astype(BF16)
        term = _dot(a, hi) if a_exact else _dot(hi, b)
        out = term if out is None else out + term
        rest = rest - hi.astype(F32)
    return out


def _dot_nt(a, b):
    return lax.dot_general(a, b, (((1,), (1,)), ((), ())), preferred_element_type=F32)


def _dot_tn(a, b):
    return lax.dot_general(a, b, (((0,), (0,)), ((), ())), preferred_element_type=F32)


def _norm_kernel(x_ref, g_ref, o_ref):
    o_ref[...] = _rms_rows(x_ref[...], g_ref[...]).astype(BF16)


def rmsnorm_bf16(x, g):
    m, d = x.shape
    tm = min(m, 512)
    return pl.pallas_call(
        _norm_kernel,
        out_shape=jax.ShapeDtypeStruct((m, d), BF16),
        grid=(m // tm,),
        in_specs=[pl.BlockSpec((tm, d), lambda i: (i, 0)), pl.BlockSpec((1, d), lambda i: (0, 0))],
        out_specs=pl.BlockSpec((tm, d), lambda i: (i, 0)),
        compiler_params=_params("parallel"),
        name="rmsnorm_bf16",
    )(x, g)


def _proj_ssd_kernel(n_ref, wzx_ref, wd_ref, bd_ref, z_ref, xbc_ref, dt_ref):
    n = n_ref[...]
    zx = _dot_nt(n, wzx_ref[0])
    z_ref[...] = zx[:, :SSD_INNER]
    xbc_ref[...] = zx[:, SSD_INNER:]
    dt_ref[...] = _softplus(_dot_nt(n, wd_ref[0]) + bd_ref[...])


def proj_ssd(n, w_r, bd, *, layer):
    m, d = n.shape
    tm = min(m, 512)
    row = lambda w: pl.BlockSpec((tm, w), lambda i: (i, 0))
    zx_rows = SSD_INNER + SSD_CONV_DIM
    return pl.pallas_call(
        _proj_ssd_kernel,
        out_shape=(jax.ShapeDtypeStruct((m, SSD_INNER), F32),
                   jax.ShapeDtypeStruct((m, SSD_CONV_DIM), F32),
                   jax.ShapeDtypeStruct((m, LANES), F32)),
        grid=(m // tm,),
        in_specs=[row(d), _layer_block((zx_rows, d), layer, _ROW_Z // zx_rows),
                  _layer_block((LANES, d), layer, _ROW_DT // LANES), _resident(bd.shape)],
        out_specs=(row(SSD_INNER), row(SSD_CONV_DIM), row(LANES)),
        compiler_params=_params("parallel"),
        name="proj_ssd",
    )(n, w_r, w_r, bd)


def _proj_ccq_kernel(n_ref, wg_ref, wq_ref, qg_ref, sel_ref, selt_ref, u_ref, q_ref):
    n = n_ref[...]
    glu = _dot_nt(n, wg_ref[0])
    u_ref[...] = glu[:, :CC_CH] * jax.nn.sigmoid(glu[:, CC_CH:])
    q = _dot_nt(n, wq_ref[0])
    ms = _dot_split(q * q, selt_ref[...], 2) * (1.0 / FOX_HEAD_DIM)
    r = _dot_split(lax.rsqrt(ms + EPS), sel_ref[...], 2)
    q_ref[...] = (q * r * qg_ref[...]).astype(BF16)


def proj_ccq(n, w_r, qg, sel, selt, *, layer):
    m, d = n.shape
    tm = min(m, 512)
    row = lambda w: pl.BlockSpec((tm, w), lambda i: (i, 0))
    return pl.pallas_call(
        _proj_ccq_kernel,
        out_shape=(jax.ShapeDtypeStruct((m, CC_CH), F32), jax.ShapeDtypeStruct((m, FOX_WIDTH), BF16)),
        grid=(m // tm,),
        in_specs=[row(d), _layer_block((2 * CC_CH, d), layer, _ROW_GLU // (2 * CC_CH)),
                  _layer_block((FOX_WIDTH, d), layer, _ROW_Q // FOX_WIDTH), _resident(qg.shape),
                  _resident(sel.shape), _resident(selt.shape)],
        out_specs=(row(CC_CH), row(FOX_WIDTH)),
        compiler_params=_params("parallel"),
        name="proj_ccq",
    )(n, w_r, w_r, qg, sel, selt)


def _proj_t_kernel(n_ref, wk_ref, wv_ref, wf_ref, gk_ref, bf_ref, kt_ref, vt_ref, lf_ref, cum_ref, carry_ref,
                   *, tm, seg):
    j = pl.program_id(1)
    n = n_ref[...]
    kt = _dot_nt(wk_ref[0], n)
    k3 = kt.reshape(FOX_HEADS, FOX_HEAD_DIM, tm)
    ms = jnp.mean(k3 * k3, axis=1, keepdims=True)
    k3 = k3 * lax.rsqrt(ms + EPS)
    kt_ref[0] = k3.reshape(FOX_WIDTH, tm) * jnp.tile(gk_ref[...], (1, tm // gk_ref.shape[1]))
    vt_ref[0] = _dot_nt(wv_ref[0], n)
    raw = _dot_nt(wf_ref[0], n)[:HEAD_ROWS] + jnp.tile(bf_ref[...], (1, tm // bf_ref.shape[1]))
    lf = -_softplus(-raw)
    lf_ref[0] = lf

    @pl.when((j * tm) % seg == 0)
    def _():
        carry_ref[...] = jnp.zeros_like(carry_ref)

    s_idx = lax.broadcasted_iota(jnp.int32, (tm, tm), 0)
    t_idx = lax.broadcasted_iota(jnp.int32, (tm, tm), 1)
    upper = s_idx <= t_idx
    if seg % tm != 0:
        upper = upper & ((s_idx // seg) == (t_idx // seg))
    cum = _dot_split(lf, jnp.where(upper, 1.0, 0.0).astype(BF16))
    carry = carry_ref[...]
    cum_ref[0] = cum + jnp.tile(carry, (1, tm // LANES)) if tm >= LANES else cum + carry[:, :tm]
    carry_ref[...] = carry + _dot_split(lf, jnp.ones((tm, LANES), BF16))


def proj_t(n, w_r, gk, bf, *, layer, nb, seq, seg):
    m, d = n.shape
    tm = min(seq, 512)
    assert seg % tm == 0 or seq == tm
    nj = seq // tm
    lane = min(tm, LANES)
    gk = gk[:, :lane]
    bf = bf[:, :lane]
    big = lambda rows: pl.BlockSpec((1, rows, tm), lambda b, j: (b, 0, j))
    return pl.pallas_call(
        functools.partial(_proj_t_kernel, tm=tm, seg=seg),
        out_shape=(jax.ShapeDtypeStruct((nb, FOX_WIDTH, seq), F32),
                   jax.ShapeDtypeStruct((nb, FOX_WIDTH, seq), F32),
                   jax.ShapeDtypeStruct((nb, HEAD_ROWS, seq), F32),
                   jax.ShapeDtypeStruct((nb, HEAD_ROWS, seq), F32)),
        grid=(nb, nj),
        in_specs=[pl.BlockSpec((tm, d), lambda b, j: (b * nj + j, 0)),
                  _layer_block((FOX_WIDTH, d), layer, _ROW_K // FOX_WIDTH),
                  _layer_block((FOX_WIDTH, d), layer, _ROW_V // FOX_WIDTH),
                  _layer_block((LANES, d), layer, _ROW_FG // LANES),
                  _resident(gk.shape), _resident(bf.shape)],
        out_specs=(big(FOX_WIDTH), big(FOX_WIDTH), big(HEAD_ROWS), big(HEAD_ROWS)),
        scratch_shapes=[pltpu.VMEM((HEAD_ROWS, LANES), F32)],
        compiler_params=_params("parallel", "arbitrary"),
        name="proj_t",
    )(n, w_r, w_r, w_r, gk, bf)


def _ssd_kernel(z_ref, xbc_ref, dt_ref, buf_ref, h0_ref, cw_ref, cb_ref, alog_ref, dl_ref, gn_ref, sel_ref,
                y_ref, nbuf_ref, hl_ref, ext_ref, h_ref, yg_ref, *, nc, valid):
    c = pl.program_id(1)
    q = SSD_CHUNK
    pad = 8
    c_last = (valid - 1) // q
    t_last = (valid - 1) % q

    @pl.when(c == 0)
    def _():
        ext_ref[pl.ds(pad - 3, 3), :] = buf_ref[0]
        h_ref[...] = h0_ref[0]

    ext_ref[pl.ds(pad, q), :] = xbc_ref[...]
    u = jnp.zeros((q, SSD_CONV_DIM), F32) + cb_ref[...]
    for j in range(SSD_CONV):
        u = u + ext_ref[pl.ds(pad - 3 + j, q), :] * cw_ref[pl.ds(j, 1), :]
    u = _silu(u)

    @pl.when(c == c_last)
    def _():
        nbuf_ref[0] = ext_ref[pl.ds(pad + t_last - 2, 3), :]

    ext_ref[pl.ds(pad - 3, 3), :] = ext_ref[pl.ds(pad + q - 3, 3), :]

    row = lax.broadcasted_iota(jnp.int32, (q, LANES), 0)
    col = lax.broadcasted_iota(jnp.int32, (q, LANES), 1)
    dt = jnp.where(c * q + row < valid, dt_ref[...], 0.0)
    a = -jnp.exp(alog_ref[...])
    tril = row >= col
    cum = _dot_split(jnp.where(tril, 1.0, 0.0).astype(BF16), dt * a)
    sel = sel_ref[...]
    cumw = _dot_split(cum, sel)
    dtw = _dot_split(dt, sel)
    cum_t = cum.T
    dt_t = dt.T
    lastw = cumw[q - 1:q, :]
    w_in_state = jnp.exp(lastw - cumw) * dtw
    e_cumw = jnp.exp(cumw)
    e_last = jnp.exp(cum[q - 1:q, :])
    lane_lo = col < SSD_HEAD_DIM
    row_lo = row < SSD_HEAD_DIM

    hpg = SSD_HEADS // SSD_GROUPS
    ssq = jnp.zeros((q, 1), F32)
    for g in range(SSD_GROUPS):
        bm = u[:, SSD_INNER + g * SSD_STATE:SSD_INNER + (g + 1) * SSD_STATE]
        cm = u[:, SSD_INNER + (SSD_GROUPS + g) * SSD_STATE:SSD_INNER + (SSD_GROUPS + g + 1) * SSD_STATE]
        bm16 = bm.astype(BF16)
        cm16 = cm.astype(BF16)
        cb = _dot_nt(cm16, bm16)
        for pp in range(hpg // 2):
            p = g * (hpg // 2) + pp
            lo, hi = p * LANES, (p + 1) * LANES
            xs = u[:, lo:hi]
            xs16 = xs.astype(BF16)
            yd = []
            for hh in range(2):
                h = 2 * p + hh
                seg = cum[:, h:h + 1] - cum_t[h:h + 1, :]
                decay = jnp.exp(jnp.where(tril, seg, -jnp.inf))
                mm = cb * decay * dt_t[h:h + 1, :]
                yd.append(_dot(mm.astype(BF16), xs16))
            y = jnp.where(lane_lo, yd[0], yd[1])
            h_pair = h_ref[pl.ds(lo, LANES), :]
            y = y + _dot_nt(cm16, h_pair.astype(BF16)) * e_cumw[:, lo:hi]
            y = y + dl_ref[:, lo:hi] * xs
            cs = _dot_tn((xs * w_in_state[:, lo:hi]).astype(BF16), bm16)
            dec = jnp.where(row_lo, e_last[:, 2 * p:2 * p + 1], e_last[:, 2 * p + 1:2 * p + 2])
            h_ref[pl.ds(lo, LANES), :] = dec * h_pair + cs
            yg = y * _silu(z_ref[:, lo:hi])
            yg_ref[:, lo:hi] = yg
            ssq = ssq + jnp.sum(yg * yg, axis=-1, keepdims=True)
    scale = lax.rsqrt(ssq * (1.0 / SSD_INNER) + EPS)
    y_ref[...] = (yg_ref[...] * scale * gn_ref[...]).astype(BF16)

    @pl.when(c == nc - 1)
    def _():
        hl_ref[0] = h_ref[...]


def ssd_mixer(z, xbc, dt, buf, h0, cw, cb, alog, dlane, gn, sel, *, nb, seq, valid):
    q = SSD_CHUNK
    nc = seq // q
    row = lambda w: pl.BlockSpec((q, w), lambda b, c: (b * nc + c, 0))
    return pl.pallas_call(
        functools.partial(_ssd_kernel, nc=nc, valid=valid),
        out_shape=(jax.ShapeDtypeStruct((nb * seq, SSD_INNER), BF16),
                   jax.ShapeDtypeStruct((nb, SSD_CONV - 1, SSD_CONV_DIM), F32),
                   jax.ShapeDtypeStruct((nb, SSD_INNER, SSD_STATE), F32)),
        grid=(nb, nc),
        in_specs=[row(SSD_INNER), row(SSD_CONV_DIM), row(LANES),
                  pl.BlockSpec((1, SSD_CONV - 1, SSD_CONV_DIM), lambda b, c: (b, 0, 0)),
                  pl.BlockSpec((1, SSD_INNER, SSD_STATE), lambda b, c: (b, 0, 0)),
                  _resident(cw.shape), _resident(cb.shape), _resident(alog.shape), _resident(dlane.shape),
                  _resident(gn.shape), _resident(sel.shape)],
        out_specs=(row(SSD_INNER),
                   pl.BlockSpec((1, SSD_CONV - 1, SSD_CONV_DIM), lambda b, c: (b, 0, 0)),
                   pl.BlockSpec((1, SSD_INNER, SSD_STATE), lambda b, c: (b, 0, 0))),
        scratch_shapes=[pltpu.VMEM((q + 8, SSD_CONV_DIM), F32),
                        pltpu.VMEM((SSD_INNER, SSD_STATE), F32),
                        pltpu.VMEM((q, SSD_INNER), F32)],
        compiler_params=_params("parallel", "arbitrary"),
        name="ssd_mixer",
    )(z, xbc, dt, buf, h0, cw, cb, alog, dlane, gn, sel)


def _cc_kernel(u_ref, buf_ref, w_ref, b_ref, g_ref, bb_ref, y_ref, nbuf_ref, ext_ref, *, tt, nt):
    j = pl.program_id(1)
    hist = CC_WIDTH - 1
    pad = 32
    rb = min(tt, 32)

    @pl.when(j == 0)
    def _():
        ext_ref[pl.ds(pad - hist, hist), :] = buf_ref[0]

    ext_ref[pl.ds(pad, tt), :] = u_ref[...]
    for r0 in range(0, tt, rb):
        acc = jnp.zeros((rb, CC_CH), F32) + b_ref[...]
        for k in range(CC_WIDTH):
            acc = acc + ext_ref[pl.ds(pad - hist + k + r0, rb), :] * w_ref[pl.ds(k, 1), :]
        mu = jnp.mean(acc, axis=-1, keepdims=True)
        xc = acc - mu
        v = xc * lax.rsqrt(jnp.mean(xc * xc, axis=-1, keepdims=True) + EPS) * g_ref[...] + bb_ref[...]
        y_ref[pl.ds(r0, rb), :] = _silu(v).astype(BF16)

    @pl.when(j == nt - 1)
    def _():
        nbuf_ref[0] = ext_ref[pl.ds(pad + tt - hist, hist), :]

    if nt > 1:
        ext_ref[pl.ds(pad - hist, hist), :] = ext_ref[pl.ds(pad + tt - hist, hist), :]


def cc_mixer(u, buf, w, b, g, bb, *, nb, seq):
    tt = min(seq, 128)
    nt = seq // tt
    assert nt == 1 or tt >= CC_WIDTH - 1
    return pl.pallas_call(
        functools.partial(_cc_kernel, tt=tt, nt=nt),
        out_shape=(jax.ShapeDtypeStruct((nb * seq, CC_CH), BF16),
                   jax.ShapeDtypeStruct((nb, CC_WIDTH - 1, CC_CH), F32)),
        grid=(nb, nt),
        in_specs=[pl.BlockSpec((tt, CC_CH), lambda i, j: (i * nt + j, 0)),
                  pl.BlockSpec((1, CC_WIDTH - 1, CC_CH), lambda i, j: (i, 0, 0)),
                  _resident(w.shape), _resident(b.shape), _resident(g.shape), _resident(bb.shape)],
        out_specs=(pl.BlockSpec((tt, CC_CH), lambda i, j: (i * nt + j, 0)),
                   pl.BlockSpec((1, CC_WIDTH - 1, CC_CH), lambda i, j: (i, 0, 0))),
        scratch_shapes=[pltpu.VMEM((32 + tt, CC_CH), F32)],
        compiler_params=_params("parallel", "arbitrary"),
        name="cc_mixer",
    )(u, buf, w, b, g, bb)


def _fox_prompt_kernel(q_ref, kt_ref, vt_ref, ct_ref, o_ref, k16_ref, v16_ref, *, tq, nq):
    hp = pl.program_id(1)
    i = pl.program_id(2)

    @pl.when(i == 0)
    def _():
        k16_ref[...] = kt_ref[0].astype(BF16)
        v16_ref[...] = vt_ref[0].astype(BF16)

    def attend(ii):
        w = (ii + 1) * tq
        q = q_ref[...]
        lane = lax.broadcasted_iota(jnp.int32, (tq, LANES), 1)
        keep = (lax.broadcasted_iota(jnp.int32, (tq, tq), 1) <= lax.broadcasted_iota(jnp.int32, (tq, tq), 0))
        k16 = k16_ref[:, :w]
        v16 = v16_ref[:, :w]
        outs = []
        for hh in range(2):
            qh = jnp.where((lane >= hh * FOX_HEAD_DIM) & (lane < (hh + 1) * FOX_HEAD_DIM), q, jnp.zeros_like(q))
            s = _dot(qh, k16) - ct_ref[0, pl.ds(2 * hp + hh, 1), :w]
            tail = jnp.where(keep, s[:, w - tq:], -jnp.inf)
            s = tail if ii == 0 else jnp.concatenate([s[:, :w - tq], tail], axis=1)
            p = jnp.exp(s - jnp.max(s, axis=-1, keepdims=True))
            l = jnp.sum(p, axis=-1, keepdims=True)
            outs.append(_dot_nt(p.astype(BF16), v16) / l)
        o_ref[...] = jnp.where(lane < FOX_HEAD_DIM, outs[0], outs[1]).astype(BF16)

    for ii in range(nq):
        pl.when(i == ii)(functools.partial(attend, ii))


def fox_prompt(q, kt, vt, ct, *, nb, seq):
    tq = min(seq, 256)
    nq = seq // tq
    return pl.pallas_call(
        functools.partial(_fox_prompt_kernel, tq=tq, nq=nq),
        out_shape=jax.ShapeDtypeStruct((nb * seq, FOX_WIDTH), BF16),
        grid=(nb, FOX_HEADS // 2, nq),
        in_specs=[pl.BlockSpec((tq, LANES), lambda b, h, i: (b * nq + i, h)),
                  pl.BlockSpec((1, LANES, seq), lambda b, h, i: (b, h, 0)),
                  pl.BlockSpec((1, LANES, seq), lambda b, h, i: (b, h, 0)),
                  pl.BlockSpec((1, HEAD_ROWS, seq), lambda b, h, i: (b, 0, 0))],
        out_specs=pl.BlockSpec((tq, LANES), lambda b, h, i: (b * nq + i, h)),
        scratch_shapes=[pltpu.VMEM((LANES, seq), BF16), pltpu.VMEM((LANES, seq), BF16)],
        compiler_params=_params("parallel", "parallel", "arbitrary"),
        name="fox_prompt",
    )(q, kt, vt, ct)


def _fox_sample_kernel(pt_ref, qbd_ref, knew_ref, vnew_ref, lfnew_ref, *refs, layer, npg, nsteps, tnew):
    del pt_ref, layer
    k_refs = refs[:npg]
    v_refs = refs[npg:2 * npg]
    f_refs = refs[2 * npg:3 * npg]
    o_ref = refs[3 * npg]
    m_ref, l_ref, acc_ref, carry_ref, kcat_ref, vcat_ref = refs[3 * npg + 1:]
    j = pl.program_id(1)
    rows = 8 * HEAD_ROWS

    @pl.when(j == 0)
    def _():
        m_ref[...] = jnp.full(m_ref.shape, -1e30, F32)
        l_ref[...] = jnp.zeros(l_ref.shape, F32)
        acc_ref[...] = jnp.zeros(acc_ref.shape, F32)
        carry_ref[...] = jnp.zeros(carry_ref.shape, F32)

    qbd = qbd_ref[0]
    lane = lax.broadcasted_iota(jnp.int32, (HEAD_ROWS, LANES), 1)

    def prefix(lf):
        c = lf
        sh = 1
        while sh < LANES:
            c = c + jnp.where(lane >= sh, pltpu.roll(c, sh, axis=1), 0.0)
            sh *= 2
        return c

    def attend(kt16, vt16, lfs, keep):
        s_all = _dot(qbd, kt16)
        carry = carry_ref[...]
        ss = []
        for r, lf in enumerate(lfs):
            cum = prefix(lf) + carry
            carry = carry + jnp.sum(lf, axis=-1, keepdims=True)
            s = s_all[:, r * LANES:(r + 1) * LANES] - jnp.tile(cum, (rows // HEAD_ROWS, 1))
            if keep is not None:
                s = jnp.where(keep, s, -jnp.inf)
            ss.append(s)
        carry_ref[...] = carry
        m_cur = ss[0]
        for s in ss[1:]:
            m_cur = jnp.maximum(m_cur, s)
        m_prev = m_ref[...]
        m_new = jnp.maximum(m_prev, jnp.max(m_cur, axis=-1, keepdims=True))
        alpha = jnp.exp(m_prev - m_new)
        ps = [jnp.exp(s - m_new) for s in ss]
        p_sum = ps[0]
        for p in ps[1:]:
            p_sum = p_sum + p
        l_ref[...] = alpha * l_ref[...] + jnp.sum(p_sum, axis=-1, keepdims=True)
        m_ref[...] = m_new
        p16 = jnp.concatenate([p.astype(BF16) for p in ps], axis=1) if len(ps) > 1 else ps[0].astype(BF16)
        pv = _dot_nt(vt16, p16)
        acc_ref[...] = acc_ref[...] * alpha.T[0:1, :] + pv

    zpad = jnp.zeros((HEAD_ROWS - FOX_HEADS, LANES), F32)
    for r in range(npg):
        kcat_ref[:, r * LANES:(r + 1) * LANES] = k_refs[r][0, 0].astype(BF16)
        vcat_ref[:, r * LANES:(r + 1) * LANES] = v_refs[r][0, 0].astype(BF16)
    attend(kcat_ref[...], vcat_ref[...],
           [jnp.concatenate([f_refs[r][0], zpad], axis=0) for r in range(npg)], None)

    @pl.when(j == nsteps - 1)
    def _():
        srow = lax.broadcasted_iota(jnp.int32, (rows, LANES), 0)
        scol = lax.broadcasted_iota(jnp.int32, (rows, LANES), 1)
        keep = (scol <= srow // HEAD_ROWS) & (scol < tnew)
        lf_new = jnp.where(lane < tnew, lfnew_ref[0], 0.0)
        attend(knew_ref[0].astype(BF16), vnew_ref[0].astype(BF16), [lf_new], keep)
        o_t = acc_ref[...] * (1.0 / l_ref[...]).T[0:1, :]
        o = o_t.T
        orow = lax.broadcasted_iota(jnp.int32, (rows, FOX_WIDTH), 0)
        ocol = lax.broadcasted_iota(jnp.int32, (rows, FOX_WIDTH), 1)
        o = jnp.where((orow % HEAD_ROWS) == (ocol // FOX_HEAD_DIM), o, 0.0)
        pick = (lax.broadcasted_iota(jnp.int32, (8, rows), 1) // HEAD_ROWS
                == lax.broadcasted_iota(jnp.int32, (8, rows), 0)).astype(F32)
        o_ref[0] = _dot_hi(pick, o).astype(BF16)


def fox_sample(qbd, knew, vnew, lfnew, kcache, vcache, fcache, page_table, *, layer, tnew):
    nb, n_pages = page_table.shape
    npg = min(PAGES_PER_STEP, n_pages)
    nsteps = n_pages // npg
    assert tnew <= 8

    def page_spec(r, shape, kind):
        if kind == "kv":
            return pl.BlockSpec((1, 1, FOX_WIDTH, PAGE_SIZE), lambda b, j, pt: (layer, pt[b, j * npg + r], 0, 0))
        return pl.BlockSpec((1, FOX_HEADS, PAGE_SIZE), lambda b, j, pt: (layer, 0, pt[b, j * npg + r]))

    per_seq = lambda shape: pl.BlockSpec((1,) + shape, lambda b, j, pt: (b, 0, 0))
    in_specs = [per_seq((8 * HEAD_ROWS, FOX_WIDTH)), per_seq((FOX_WIDTH, LANES)), per_seq((FOX_WIDTH, LANES)),
                per_seq((HEAD_ROWS, LANES))]
    in_specs += [page_spec(r, None, "kv") for r in range(npg)]
    in_specs += [page_spec(r, None, "kv") for r in range(npg)]
    in_specs += [page_spec(r, None, "f") for r in range(npg)]
    grid_spec = pltpu.PrefetchScalarGridSpec(
        num_scalar_prefetch=1,
        grid=(nb, nsteps),
        in_specs=in_specs,
        out_specs=pl.BlockSpec((1, 8, FOX_WIDTH), lambda b, j, pt: (b, 0, 0)),
        scratch_shapes=[pltpu.VMEM((8 * HEAD_ROWS, LANES), F32), pltpu.VMEM((8 * HEAD_ROWS, LANES), F32),
                        pltpu.VMEM((FOX_WIDTH, LANES), F32), pltpu.VMEM((HEAD_ROWS, LANES), F32),
                        pltpu.VMEM((FOX_WIDTH, npg * PAGE_SIZE), BF16),
                        pltpu.VMEM((FOX_WIDTH, npg * PAGE_SIZE), BF16)],
    )
    return pl.pallas_call(
        functools.partial(_fox_sample_kernel, layer=layer, npg=npg, nsteps=nsteps, tnew=tnew),
        out_shape=jax.ShapeDtypeStruct((nb, 8, FOX_WIDTH), BF16),
        grid_spec=grid_spec,
        compiler_params=_params("parallel", "arbitrary"),
        name="fox_sample",
    )(page_table, qbd, knew, vnew, lfnew, *([kcache] * npg), *([vcache] * npg), *([fcache] * npg))


def _outproj_kernel(ys_ref, yc_ref, yf_ref, w_ref, x_ref, g_ref, xo_ref, no_ref):
    a = SSD_INNER
    b = SSD_INNER + CC_CH
    acc = _dot(ys_ref[...], w_ref[0, 0:a, :])
    acc = acc + _dot(yc_ref[...], w_ref[0, a:b, :])
    acc = acc + _dot(yf_ref[...], w_ref[0, b:, :])
    xn = x_ref[...] + acc
    xo_ref[...] = xn
    no_ref[...] = _rms_rows(xn, g_ref[...]).astype(BF16)


def out_proj(ys, yc, yf, w, x, g, *, layer):
    m, d = x.shape
    tm = min(m, 512)
    row = lambda wd: pl.BlockSpec((tm, wd), lambda i: (i, 0))
    return pl.pallas_call(
        _outproj_kernel,
        out_shape=(jax.ShapeDtypeStruct((m, d), F32), jax.ShapeDtypeStruct((m, d), BF16)),
        grid=(m // tm,),
        in_specs=[row(SSD_INNER), row(CC_CH), row(FOX_WIDTH), _layer_block(w.shape[1:], layer), row(d),
                  _resident(g.shape)],
        out_specs=(row(d), row(d)),
        compiler_params=_params("parallel"),
        name="out_proj",
    )(ys, yc, yf, w, x, g)


def _memkv_kernel(m_ref, gm_ref, wk_ref, wv_ref, gk_ref, k_ref, v_ref):
    mn = _rms_rows(m_ref[...], gm_ref[...]).astype(BF16)
    k = _dot(mn, wk_ref[0])
    for h in range(XA_HEADS):
        lo, hi = h * XA_HEAD_DIM, (h + 1) * XA_HEAD_DIM
        k_ref[:, lo:hi] = _rms_rows(k[:, lo:hi], gk_ref[...])
    v_ref[...] = _dot(mn, wv_ref[0])


def mem_kv(mem, gm, wk, wv, gk, *, layer):
    m, d = mem.shape
    tm = min(m, 256)
    row = lambda wd: pl.BlockSpec((tm, wd), lambda i: (i, 0))
    return pl.pallas_call(
        _memkv_kernel,
        out_shape=(jax.ShapeDtypeStruct((m, XA_WIDTH), F32), jax.ShapeDtypeStruct((m, XA_WIDTH), F32)),
        grid=(m // tm,),
        in_specs=[row(d), _resident(gm.shape), _layer_block(wk.shape[1:], layer), _layer_block(wv.shape[1:], layer),
                  _resident(gk.shape)],
        out_specs=(row(XA_WIDTH), row(XA_WIDTH)),
        compiler_params=_params("parallel"),
        name="mem_kv",
    )(mem, gm, wk, wv, gk)


def _xattn_kernel(n_ref, x_ref, wq_ref, qg_ref, mk_ref, mv_ref, wo_ref, g_ref, xo_ref, no_ref):
    q = _dot(n_ref[...], wq_ref[0])
    mk = mk_ref[0].astype(BF16)
    mv = mv_ref[0].astype(BF16)
    outs = []
    for h in range(XA_HEADS):
        lo, hi = h * XA_HEAD_DIM, (h + 1) * XA_HEAD_DIM
        qh = _rms_rows(q[:, lo:hi], qg_ref[...]).astype(BF16)
        s = _dot_nt(qh, mk[:, lo:hi]) * (XA_HEAD_DIM ** -0.5)
        s = s - jnp.max(s, axis=-1, keepdims=True)
        e = jnp.exp(s)
        p = e / jnp.sum(e, axis=-1, keepdims=True)
        outs.append(_dot(p.astype(BF16), mv[:, lo:hi]))
    o = jnp.concatenate(outs, axis=-1).astype(BF16)
    xn = x_ref[...] + _dot(o, wo_ref[0])
    xo_ref[...] = xn
    no_ref[...] = _rms_rows(xn, g_ref[...]).astype(BF16)


def cross_attn(n, x, wq, qg, mk, mv, wo, g, *, layer, seq):
    m, d = x.shape
    tm = min(seq, 512)
    per = seq // tm
    mem = mk.shape[1]
    row = lambda wd: pl.BlockSpec((tm, wd), lambda i: (i, 0))
    kv = pl.BlockSpec((1, mem, XA_WIDTH), lambda i: (i // per, 0, 0))
    return pl.pallas_call(
        _xattn_kernel,
        out_shape=(jax.ShapeDtypeStruct((m, d), F32), jax.ShapeDtypeStruct((m, d), BF16)),
        grid=(m // tm,),
        in_specs=[row(d), row(d), _layer_block(wq.shape[1:], layer), _resident(qg.shape), kv, kv,
                  _layer_block(wo.shape[1:], layer), _resident(g.shape)],
        out_specs=(row(d), row(d)),
        compiler_params=_params("parallel"),
        name="cross_attn",
    )(n, x, wq, qg, mk, mv, wo, g)


def _mix_xattn_kernel(ys_ref, yc_ref, yf_ref, w_ref, x_ref, gx_ref, wq_ref, qg_ref, mk_ref, mv_ref, wo_ref, g_ref,
                      xo_ref, no_ref):
    a = SSD_INNER
    b = SSD_INNER + CC_CH
    acc = _dot(ys_ref[...], w_ref[0, 0:a, :])
    acc = acc + _dot(yc_ref[...], w_ref[0, a:b, :])
    acc = acc + _dot(yf_ref[...], w_ref[0, b:, :])
    x1 = x_ref[...] + acc
    n1 = _rms_rows(x1, gx_ref[...]).astype(BF16)
    q = _dot(n1, wq_ref[0])
    mk = mk_ref[0].astype(BF16)
    mv = mv_ref[0].astype(BF16)
    outs = []
    for h in range(XA_HEADS):
        lo, hi = h * XA_HEAD_DIM, (h + 1) * XA_HEAD_DIM
        qh = _rms_rows(q[:, lo:hi], qg_ref[...]).astype(BF16)
        s = _dot_nt(qh, mk[:, lo:hi]) * (XA_HEAD_DIM ** -0.5)
        s = s - jnp.max(s, axis=-1, keepdims=True)
        e = jnp.exp(s)
        p = e / jnp.sum(e, axis=-1, keepdims=True)
        outs.append(_dot(p.astype(BF16), mv[:, lo:hi]))
    o = jnp.concatenate(outs, axis=-1).astype(BF16)
    x2 = x1 + _dot(o, wo_ref[0])
    xo_ref[...] = x2
    no_ref[...] = _rms_rows(x2, g_ref[...]).astype(BF16)


def mix_xattn(ys, yc, yf, w, x, gx, wq, qg, mk, mv, wo, g, *, layer, seq):
    m, d = x.shape
    tm = min(seq, 512)
    per = seq // tm
    mem = mk.shape[1]
    row = lambda wd: pl.BlockSpec((tm, wd), lambda i: (i, 0))
    kv = pl.BlockSpec((1, mem, XA_WIDTH), lambda i: (i // per, 0, 0))
    return pl.pallas_call(
        _mix_xattn_kernel,
        out_shape=(jax.ShapeDtypeStruct((m, d), F32), jax.ShapeDtypeStruct((m, d), BF16)),
        grid=(m // tm,),
        in_specs=[row(SSD_INNER), row(CC_CH), row(FOX_WIDTH), _layer_block(w.shape[1:], layer), row(d),
                  _resident(gx.shape), _layer_block(wq.shape[1:], layer), _resident(qg.shape), kv, kv,
                  _layer_block(wo.shape[1:], layer), _resident(g.shape)],
        out_specs=(row(d), row(d)),
        compiler_params=_params("parallel"),
        name="mix_xattn",
    )(ys, yc, yf, w, x, gx, wq, qg, mk, mv, wo, g)


def _ffn_kernel(n_ref, x_ref, wg_ref, wu_ref, wd_ref, g_ref, xo_ref, no_ref, *, nf):
    f = pl.program_id(1)

    @pl.when(f == 0)
    def _():
        xo_ref[...] = x_ref[...]

    n = n_ref[...]
    a = (_silu(_dot(n, wg_ref[0, 0])) * _dot(n, wu_ref[0, 0])).astype(BF16)
    xo_ref[...] += _dot(a, wd_ref[0])

    @pl.when(f == nf - 1)
    def _():
        no_ref[...] = _rms_rows(xo_ref[...], g_ref[...]).astype(BF16)


def ffn(n, x, wg, wu, wd, g, *, layer):
    m, d = x.shape
    nf, tf = wg.shape[1], wg.shape[3]
    tm = min(m, 1024)
    row = lambda wdt: pl.BlockSpec((tm, wdt), lambda i, f: (i, 0))
    x_spec = pl.BlockSpec((tm, d), lambda i, f: (i, 0), pipeline_mode=pl.Buffered(1))
    return pl.pallas_call(
        functools.partial(_ffn_kernel, nf=nf),
        out_shape=(jax.ShapeDtypeStruct((m, d), F32), jax.ShapeDtypeStruct((m, d), BF16)),
        grid=(m // tm, nf),
        in_specs=[row(d), x_spec,
                  pl.BlockSpec((1, 1, d, tf), lambda i, f: (layer, f, 0, 0)),
                  pl.BlockSpec((1, 1, d, tf), lambda i, f: (layer, f, 0, 0)),
                  pl.BlockSpec((1, tf, d), lambda i, f: (layer, f, 0)), _resident(g.shape)],
        out_specs=(row(d), row(d)),
        compiler_params=_params("parallel", "arbitrary"),
        name="ffn",
    )(n, x, wg, wu, wd, g)


def _head_selector():
    r = jnp.arange(LANES)[:, None]
    c = jnp.arange(FOX_WIDTH)[None, :]
    return (r == c // FOX_HEAD_DIM).astype(BF16)


def _pad_lanes(v, width=LANES):
    return jnp.pad(v, (0, width - v.shape[0]))[None, :]


def kernel(x_prompt, x_sample, mem_prompt, cache_fox_k, cache_fox_v, cache_fox_logf, page_table, state_ssd, state_ssd_conv, state_cc_conv, cache_mem_k, cache_mem_v, norm_mix, w_in, ssd_conv_w, ssd_conv_b, ssd_dt_bias, ssd_a_log, ssd_d, ssd_norm, cc_dw_w, cc_dw_b, cc_ln_g, cc_ln_b, fox_q_norm, fox_k_norm, fox_fg_bias, w_out, norm_xa, norm_mem, xa_wq, xa_wk, xa_wv, xa_q_norm, xa_k_norm, xa_wo, norm_ffn, ffn_wg, ffn_wu, ffn_wd):
    depth = norm_mix.shape[0]
    bp, lp, d = x_prompt.shape
    bs, ls, _ = x_sample.shape
    mp, ms = bp * lp, bs * ls
    n_pool = cache_fox_k.shape[1]
    mem_len = mem_prompt.shape[1]
    q = SSD_CHUNK

    sel = _head_selector()
    selt = sel.T
    kcache = jnp.transpose(cache_fox_k, (0, 1, 3, 4, 2)).reshape(depth, n_pool, FOX_WIDTH, PAGE_SIZE)
    vcache = jnp.transpose(cache_fox_v, (0, 1, 3, 4, 2)).reshape(depth, n_pool, FOX_WIDTH, PAGE_SIZE)
    fcache = jnp.transpose(cache_fox_logf, (0, 3, 1, 2)).reshape(depth, FOX_HEADS, n_pool * PAGE_SIZE)
    w_in_t = jnp.transpose(w_in, (0, 2, 1))
    zrows = jnp.zeros((depth, LANES - SSD_HEADS, d), F32)
    w_r = jnp.concatenate([w_in_t[:, _OFF_Z:_OFF_DT], w_in_t[:, _OFF_GLU:_OFF_FG],
                           w_in_t[:, _OFF_DT:_OFF_GLU], zrows, w_in_t[:, _OFF_FG:], zrows],
                          axis=1).astype(BF16)
    w_o = w_out.astype(BF16)
    wxq, wxk, wxv, wxo = (w.astype(BF16) for w in (xa_wq, xa_wk, xa_wv, xa_wo))
    nf = ffn_wg.shape[2] // FFN_TILE
    tile_major = lambda w: jnp.transpose(w.astype(BF16).reshape(depth, d, nf, FFN_TILE), (0, 2, 1, 3))
    wg, wu, wd = tile_major(ffn_wg), tile_major(ffn_wu), ffn_wd.astype(BF16)

    xp = x_prompt.reshape(mp, d)
    xs = x_sample.reshape(ms, d)
    mem = mem_prompt.reshape(bp * mem_len, d)
    np_ = rmsnorm_bf16(xp, norm_mix[0][None, :])
    ns_ = rmsnorm_bf16(xs, norm_mix[0][None, :])

    zeros_ssd_buf = jnp.zeros((bp, SSD_CONV - 1, SSD_CONV_DIM), F32)
    zeros_ssd_h = jnp.zeros((bp, SSD_INNER, SSD_STATE), F32)
    zeros_cc_buf = jnp.zeros((bp, CC_WIDTH - 1, CC_CH), F32)

    outs = [[] for _ in range(14)]
    for i in range(depth):
        bd = _pad_lanes(ssd_dt_bias[i])
        qg = (jnp.tile(fox_q_norm[i], FOX_HEADS) * (FOX_HEAD_DIM ** -0.5))[None, :]
        gk = jnp.broadcast_to(jnp.tile(fox_k_norm[i], FOX_HEADS)[:, None], (FOX_WIDTH, LANES))
        bfg = jnp.broadcast_to(jnp.pad(fox_fg_bias[i], (0, HEAD_ROWS - FOX_HEADS))[:, None], (HEAD_ROWS, LANES))
        alog = _pad_lanes(ssd_a_log[i])
        dlane = jnp.repeat(ssd_d[i], SSD_HEAD_DIM)[None, :]
        gn = ssd_norm[i][None, :]
        g_next = norm_mix[i + 1][None, :] if i + 1 < depth else jnp.ones((1, d), F32)
        conv_args = (ssd_conv_w[i], ssd_conv_b[i][None, :], alog, dlane, gn, sel)
        cc_args = (cc_dw_w[i], cc_dw_b[i][None, :], cc_ln_g[i][None, :], cc_ln_b[i][None, :])

        mk, mv = mem_kv(mem, norm_mem[i][None, :], wxk, wxv, xa_k_norm[i][None, :], layer=i)
        z, xbc, dt = proj_ssd(np_, w_r, bd, layer=i)
        ucc, qn = proj_ccq(np_, w_r, qg, sel, selt, layer=i)
        kt, vt, lft, cumt = proj_t(np_, w_r, gk, bfg, layer=i, nb=bp, seq=lp, seg=lp)
        y_ssd, ssd_buf, ssd_h = ssd_mixer(z, xbc, dt, zeros_ssd_buf, zeros_ssd_h, *conv_args,
                                          nb=bp, seq=lp, valid=lp)
        y_cc, cc_buf = cc_mixer(ucc, zeros_cc_buf, *cc_args, nb=bp, seq=lp)
        y_fox = fox_prompt(qn, kt, vt, cumt, nb=bp, seq=lp)
        xp, nff = mix_xattn(y_ssd, y_cc, y_fox, w_o, xp, norm_xa[i][None, :], wxq, xa_q_norm[i][None, :],
                            mk.reshape(bp, mem_len, XA_WIDTH), mv.reshape(bp, mem_len, XA_WIDTH), wxo,
                            norm_ffn[i][None, :], layer=i, seq=lp)
        xp, np_ = ffn(nff, xp, wg, wu, wd, g_next, layer=i)
        outs[0].append(jnp.transpose(kt.reshape(bp, FOX_HEADS, FOX_HEAD_DIM, lp), (0, 3, 1, 2)))
        outs[1].append(jnp.transpose(vt.reshape(bp, FOX_HEADS, FOX_HEAD_DIM, lp), (0, 3, 1, 2)))
        outs[2].append(jnp.transpose(lft[:, :FOX_HEADS, :], (0, 2, 1)))
        outs[3].append(ssd_h.reshape(bp, SSD_HEADS, SSD_HEAD_DIM, SSD_STATE))
        outs[4].append(ssd_buf)
        outs[5].append(cc_buf)
        outs[6].append(mk.reshape(bp, mem_len, XA_HEADS, XA_HEAD_DIM))
        outs[7].append(mv.reshape(bp, mem_len, XA_HEADS, XA_HEAD_DIM))

        z, xbc, dt = proj_ssd(ns_, w_r, bd, layer=i)
        ucc, qn = proj_ccq(ns_, w_r, qg, sel, selt, layer=i)
        kt, vt, lft, _ = proj_t(ns_, w_r, gk, bfg, layer=i, nb=1, seq=ms, seg=ls)
        pad_tok = lambda a: jnp.pad(a.reshape(bs, ls, a.shape[-1]), ((0, 0), (0, q - ls), (0, 0))).reshape(bs * q, a.shape[-1])
        y_ssd, ssd_buf, ssd_h = ssd_mixer(pad_tok(z), pad_tok(xbc), pad_tok(dt), state_ssd_conv[i],
                                          state_ssd[i].reshape(bs, SSD_INNER, SSD_STATE), *conv_args,
                                          nb=bs, seq=q, valid=ls)
        y_ssd = y_ssd.reshape(bs, q, SSD_INNER)[:, :ls].reshape(ms, SSD_INNER)
        y_cc, cc_buf = cc_mixer(ucc, state_cc_conv[i], *cc_args, nb=bs, seq=ls)
        q4 = qn.reshape(bs, ls, FOX_HEADS, 1, FOX_HEAD_DIM)
        eye = jnp.eye(FOX_HEADS, dtype=BF16)[None, None, :, :, None]
        qbd = (q4 * eye).reshape(bs, ls, FOX_HEADS, FOX_WIDTH)
        qbd = jnp.pad(qbd, ((0, 0), (0, 8 - ls), (0, HEAD_ROWS - FOX_HEADS), (0, 0))).reshape(bs, 8 * HEAD_ROWS, FOX_WIDTH)
        to_seq = lambda a: jnp.pad(jnp.transpose(a[0].reshape(a.shape[1], bs, ls), (1, 0, 2)),
                                   ((0, 0), (0, 0), (0, LANES - ls)))
        y_fox = fox_sample(qbd, to_seq(kt), to_seq(vt), to_seq(lft), kcache, vcache, fcache, page_table,
                           layer=i, tnew=ls)
        y_fox = y_fox[:, :ls].reshape(ms, FOX_WIDTH)
        xs, nxa = out_proj(y_ssd, y_cc, y_fox, w_o, xs, norm_xa[i][None, :], layer=i)
        xs, nff = cross_attn(nxa, xs, wxq, xa_q_norm[i][None, :], cache_mem_k[i].reshape(bs, mem_len, XA_WIDTH),
                             cache_mem_v[i].reshape(bs, mem_len, XA_WIDTH), wxo, norm_ffn[i][None, :], layer=i,
                             seq=ls)
        xs, ns_ = ffn(nff, xs, wg, wu, wd, g_next, layer=i)
        tok = lambda a, w: jnp.transpose(a[0], (1, 0)).reshape(bs, ls, w)
        outs[8].append(tok(kt, FOX_WIDTH).reshape(bs, ls, FOX_HEADS, FOX_HEAD_DIM))
        outs[9].append(tok(vt, FOX_WIDTH).reshape(bs, ls, FOX_HEADS, FOX_HEAD_DIM))
        outs[10].append(tok(lft, HEAD_ROWS)[:, :, :FOX_HEADS])
        outs[11].append(ssd_h.reshape(bs, SSD_HEADS, SSD_HEAD_DIM, SSD_STATE))
        outs[12].append(ssd_buf)
        outs[13].append(cc_buf)

    return (xp.reshape(bp, lp, d), xs.reshape(bs, ls, d)) + tuple(jnp.stack(o) for o in outs)
```

```python
import functools

import jax
import jax.numpy as jnp
from jax import lax
from jax.experimental import pallas as pl
from jax.experimental.pallas import tpu as pltpu

F32 = jnp.float32
BF16 = jnp.bfloat16
HI = lax.Precision.HIGHEST

D_MODEL = 2048
SSD_HEADS = 12
SSD_HEAD_DIM = 64
SSD_INNER = SSD_HEADS * SSD_HEAD_DIM
SSD_GROUPS = 2
SSD_STATE = 128
SSD_CONV = 4
SSD_CHUNK = 128
SSD_CONV_DIM = SSD_INNER + 2 * SSD_GROUPS * SSD_STATE
CC_CH = 512
CC_WIDTH = 31
FOX_HEADS = 12
FOX_HEAD_DIM = 64
FOX_WIDTH = FOX_HEADS * FOX_HEAD_DIM
PAGE_SIZE = 128
XA_HEADS = 4
XA_HEAD_DIM = 128
XA_WIDTH = XA_HEADS * XA_HEAD_DIM
EPS = 1e-6

LANES = 128
HEAD_ROWS = 16
VMEM_LIMIT_BYTES = 56 * 1024 * 1024
PAGES_PER_STEP = 16
FOX_PAIRS_PER_STEP = 2

_OFF_Z = 0
_OFF_XBC = _OFF_Z + SSD_INNER
_OFF_DT = _OFF_XBC + SSD_CONV_DIM
_OFF_GLU = _OFF_DT + SSD_HEADS
_OFF_Q = _OFF_GLU + 2 * CC_CH
_OFF_K = _OFF_Q + FOX_WIDTH
_OFF_V = _OFF_K + FOX_WIDTH
_OFF_FG = _OFF_V + FOX_WIDTH

_A_ROWS = _OFF_DT + LANES
_B_GLU = 0
_B_Q = _B_GLU + 2 * CC_CH
_B_K = _B_Q + FOX_WIDTH
_B_V = _B_K + FOX_WIDTH
_B_FG = _B_V + FOX_WIDTH
_B_ROWS = _B_FG + FOX_HEADS
FFN_TILE = 512


def _params(*sem):
    return pltpu.CompilerParams(dimension_semantics=sem, vmem_limit_bytes=VMEM_LIMIT_BYTES)


def _resident(shape):
    nd = len(shape)
    return pl.BlockSpec(shape, lambda *_: (0,) * nd, pipeline_mode=pl.Buffered(1))


def _layer_block(shape, layer, block=0):
    return pl.BlockSpec((1,) + tuple(shape), lambda *_: (layer, block, 0), pipeline_mode=pl.Buffered(1))


def _rms_rows(x, g):
    ms = jnp.mean(x * x, axis=-1, keepdims=True)
    return x * lax.rsqrt(ms + EPS) * g


def _softplus(x):
    return jnp.maximum(x, 0.0) + jnp.log1p(jnp.exp(-jnp.abs(x)))


def _silu(x):
    return x * jax.nn.sigmoid(x)


def _dot(a, b):
    return jnp.dot(a, b, preferred_element_type=F32)


def _dot_hi(a, b):
    return jnp.dot(a, b, precision=HI, preferred_element_type=F32)


def _dot_split(a, b, passes=3):
    a_exact = a.dtype == BF16
    rest = b if a_exact else a
    out = None
    for _ in range(passes):
        hi = rest.astype(BF16)
        term = _dot(a, hi) if a_exact else _dot(hi, b)
        out = term if out is None else out + term
        rest = rest - hi.astype(F32)
    return out


def _dot_nt(a, b):
    return lax.dot_general(a, b, (((1,), (1,)), ((), ())), preferred_element_type=F32)


def _dot_tn(a, b):
    return lax.dot_general(a, b, (((0,), (0,)), ((), ())), preferred_element_type=F32)


def _norm_kernel(x_ref, g_ref, o_ref):
    o_ref[...] = _rms_rows(x_ref[...], g_ref[...]).astype(BF16)


def rmsnorm_bf16(x, g):
    m, d = x.shape
    tm = min(m, 512)
    return pl.pallas_call(
        _norm_kernel,
        out_shape=jax.ShapeDtypeStruct((m, d), BF16),
        grid=(m // tm,),
        in_specs=[pl.BlockSpec((tm, d), lambda i: (i, 0)), pl.BlockSpec((1, d), lambda i: (0, 0))],
        out_specs=pl.BlockSpec((tm, d), lambda i: (i, 0)),
        compiler_params=_params("parallel"),
        name="rmsnorm_bf16",
    )(x, g)


def _proj_ssd_kernel(n_ref, w_ref, bd_ref, z_ref, xbc_ref, dt_ref):
    n = n_ref[...]
    zx = _dot_nt(n, w_ref[0, _OFF_Z:_OFF_DT, :])
    z_ref[...] = zx[:, :SSD_INNER]
    xbc_ref[...] = zx[:, SSD_INNER:]
    raw = _dot_nt(n, w_ref[0, _OFF_DT:_A_ROWS, :])
    lane = lax.broadcasted_iota(jnp.int32, raw.shape, 1)
    dt_ref[...] = _softplus(jnp.where(lane < SSD_HEADS, raw, 0.0) + bd_ref[...])


def proj_ssd(n, w_a, bd, *, layer):
    m, d = n.shape
    tm = min(m, 512)
    row = lambda w: pl.BlockSpec((tm, w), lambda i: (i, 0))
    return pl.pallas_call(
        _proj_ssd_kernel,
        out_shape=(jax.ShapeDtypeStruct((m, SSD_INNER), F32),
                   jax.ShapeDtypeStruct((m, SSD_CONV_DIM), F32),
                   jax.ShapeDtypeStruct((m, LANES), F32)),
        grid=(m // tm,),
        in_specs=[row(d), _layer_block((_A_ROWS, d), layer), _resident(bd.shape)],
        out_specs=(row(SSD_INNER), row(SSD_CONV_DIM), row(LANES)),
        compiler_params=_params("parallel"),
        name="proj_ssd",
    )(n, w_a, bd)


def _proj_ccq_kernel(n_ref, w_ref, qg_ref, sel_ref, selt_ref, u_ref, q_ref):
    n = n_ref[...]
    glu = _dot_nt(n, w_ref[0, _B_GLU:_B_Q, :])
    u_ref[...] = glu[:, :CC_CH] * jax.nn.sigmoid(glu[:, CC_CH:])
    q = _dot_nt(n, w_ref[0, _B_Q:_B_K, :])
    ms = _dot_split(q * q, selt_ref[...], 2) * (1.0 / FOX_HEAD_DIM)
    r = _dot_split(lax.rsqrt(ms + EPS), sel_ref[...], 2)
    q_ref[...] = (q * r * qg_ref[...]).astype(BF16)


def proj_ccq(n, w_b, qg, sel, selt, *, layer):
    m, d = n.shape
    tm = min(m, 512)
    row = lambda w: pl.BlockSpec((tm, w), lambda i: (i, 0))
    return pl.pallas_call(
        _proj_ccq_kernel,
        out_shape=(jax.ShapeDtypeStruct((m, CC_CH), F32), jax.ShapeDtypeStruct((m, FOX_WIDTH), BF16)),
        grid=(m // tm,),
        in_specs=[row(d), _layer_block((_B_K, d), layer), _resident(qg.shape),
                  _resident(sel.shape), _resident(selt.shape)],
        out_specs=(row(CC_CH), row(FOX_WIDTH)),
        compiler_params=_params("parallel"),
        name="proj_ccq",
    )(n, w_b, qg, sel, selt)


def _proj_t_kernel(n_ref, w_ref, gk_ref, bf_ref, kt_ref, vt_ref, lf_ref, cum_ref, carry_ref, *, tm, seg):
    j = pl.program_id(1)
    n = n_ref[...]
    kt = _dot_nt(w_ref[0, _B_K:_B_V, :], n)
    k3 = kt.reshape(FOX_HEADS, FOX_HEAD_DIM, tm)
    ms = jnp.mean(k3 * k3, axis=1, keepdims=True)
    k3 = k3 * lax.rsqrt(ms + EPS)
    kt_ref[0] = k3.reshape(FOX_WIDTH, tm) * jnp.tile(gk_ref[...], (1, tm // gk_ref.shape[1]))
    vt_ref[0] = _dot_nt(w_ref[0, _B_V:_B_FG, :], n)
    raw = jnp.concatenate([_dot_nt(w_ref[0, _B_FG:_B_ROWS, :], n),
                           jnp.zeros((HEAD_ROWS - FOX_HEADS, tm), F32)], axis=0)
    raw = raw + jnp.tile(bf_ref[...], (1, tm // bf_ref.shape[1]))
    lf = -_softplus(-raw)
    lf_ref[0] = lf

    @pl.when((j * tm) % seg == 0)
    def _():
        carry_ref[...] = jnp.zeros_like(carry_ref)

    s_idx = lax.broadcasted_iota(jnp.int32, (tm, tm), 0)
    t_idx = lax.broadcasted_iota(jnp.int32, (tm, tm), 1)
    upper = s_idx <= t_idx
    if seg % tm != 0:
        upper = upper & ((s_idx // seg) == (t_idx // seg))
    cum = _dot_split(lf, jnp.where(upper, 1.0, 0.0).astype(BF16))
    carry = carry_ref[...]
    cum_ref[0] = cum + jnp.tile(carry, (1, tm // LANES)) if tm >= LANES else cum + carry[:, :tm]
    carry_ref[...] = carry + _dot_split(lf, jnp.ones((tm, LANES), BF16))


def proj_t(n, w_b, gk, bf, *, layer, nb, seq, seg):
    m, d = n.shape
    tm = min(seq, 512)
    assert seg % tm == 0 or seq == tm
    nj = seq // tm
    lane = min(tm, LANES)
    gk = gk[:, :lane]
    bf = bf[:, :lane]
    big = lambda rows: pl.BlockSpec((1, rows, tm), lambda b, j: (b, 0, j))
    return pl.pallas_call(
        functools.partial(_proj_t_kernel, tm=tm, seg=seg),
        out_shape=(jax.ShapeDtypeStruct((nb, FOX_WIDTH, seq), F32),
                   jax.ShapeDtypeStruct((nb, FOX_WIDTH, seq), F32),
                   jax.ShapeDtypeStruct((nb, HEAD_ROWS, seq), F32),
                   jax.ShapeDtypeStruct((nb, HEAD_ROWS, seq), F32)),
        grid=(nb, nj),
        in_specs=[pl.BlockSpec((tm, d), lambda b, j: (b * nj + j, 0)),
                  _layer_block(w_b.shape[1:], layer), _resident(gk.shape), _resident(bf.shape)],
        out_specs=(big(FOX_WIDTH), big(FOX_WIDTH), big(HEAD_ROWS), big(HEAD_ROWS)),
        scratch_shapes=[pltpu.VMEM((HEAD_ROWS, LANES), F32)],
        compiler_params=_params("parallel", "arbitrary"),
        name="proj_t",
    )(n, w_b, gk, bf)


def _ssd_kernel(z_ref, xbc_ref, dt_ref, buf_ref, h0_ref, cw_ref, cb_ref, alog_ref, dl_ref, gn_ref, sel_ref,
                y_ref, nbuf_ref, hl_ref, ext_ref, h_ref, yg_ref, *, nc, valid):
    c = pl.program_id(1)
    q = SSD_CHUNK
    pad = 8
    c_last = (valid - 1) // q
    t_last = (valid - 1) % q

    @pl.when(c == 0)
    def _():
        ext_ref[pl.ds(pad - 3, 3), :] = buf_ref[0]
        h_ref[...] = h0_ref[0]

    ext_ref[pl.ds(pad, q), :] = xbc_ref[...]
    u = jnp.zeros((q, SSD_CONV_DIM), F32) + cb_ref[...]
    for j in range(SSD_CONV):
        u = u + ext_ref[pl.ds(pad - 3 + j, q), :] * cw_ref[pl.ds(j, 1), :]
    u = _silu(u)

    @pl.when(c == c_last)
    def _():
        nbuf_ref[0] = ext_ref[pl.ds(pad + t_last - 2, 3), :]

    ext_ref[pl.ds(pad - 3, 3), :] = ext_ref[pl.ds(pad + q - 3, 3), :]

    row = lax.broadcasted_iota(jnp.int32, (q, LANES), 0)
    col = lax.broadcasted_iota(jnp.int32, (q, LANES), 1)
    dt = jnp.where(c * q + row < valid, dt_ref[...], 0.0)
    a = -jnp.exp(alog_ref[...])
    tril = row >= col
    cum = _dot_split(jnp.where(tril, 1.0, 0.0).astype(BF16), dt * a)
    sel = sel_ref[...]
    cumw = _dot_split(cum, sel)
    dtw = _dot_split(dt, sel)
    cum_t = cum.T
    dt_t = dt.T
    lastw = cumw[q - 1:q, :]
    w_in_state = jnp.exp(lastw - cumw) * dtw
    e_cumw = jnp.exp(cumw)
    e_last = jnp.exp(cum[q - 1:q, :])
    lane_lo = col < SSD_HEAD_DIM
    row_lo = row < SSD_HEAD_DIM

    hpg = SSD_HEADS // SSD_GROUPS
    ssq = jnp.zeros((q, 1), F32)
    for g in range(SSD_GROUPS):
        bm = u[:, SSD_INNER + g * SSD_STATE:SSD_INNER + (g + 1) * SSD_STATE]
        cm = u[:, SSD_INNER + (SSD_GROUPS + g) * SSD_STATE:SSD_INNER + (SSD_GROUPS + g + 1) * SSD_STATE]
        bm16 = bm.astype(BF16)
        cm16 = cm.astype(BF16)
        cb = _dot_nt(cm16, bm16)
        for pp in range(hpg // 2):
            p = g * (hpg // 2) + pp
            lo, hi = p * LANES, (p + 1) * LANES
            xs = u[:, lo:hi]
            xs16 = xs.astype(BF16)
            yd = []
            for hh in range(2):
                h = 2 * p + hh
                seg = cum[:, h:h + 1] - cum_t[h:h + 1, :]
                decay = jnp.exp(jnp.where(tril, seg, -jnp.inf))
                mm = cb * decay * dt_t[h:h + 1, :]
                yd.append(_dot(mm.astype(BF16), xs16))
            y = jnp.where(lane_lo, yd[0], yd[1])
            h_pair = h_ref[pl.ds(lo, LANES), :]
            y = y + _dot_nt(cm16, h_pair.astype(BF16)) * e_cumw[:, lo:hi]
            y = y + dl_ref[:, lo:hi] * xs
            cs = _dot_tn((xs * w_in_state[:, lo:hi]).astype(BF16), bm16)
            dec = jnp.where(row_lo, e_last[:, 2 * p:2 * p + 1], e_last[:, 2 * p + 1:2 * p + 2])
            h_ref[pl.ds(lo, LANES), :] = dec * h_pair + cs
            yg = y * _silu(z_ref[:, lo:hi])
            yg_ref[:, lo:hi] = yg
            ssq = ssq + jnp.sum(yg * yg, axis=-1, keepdims=True)
    scale = lax.rsqrt(ssq * (1.0 / SSD_INNER) + EPS)
    y_ref[...] = (yg_ref[...] * scale * gn_ref[...]).astype(BF16)

    @pl.when(c == nc - 1)
    def _():
        hl_ref[0] = h_ref[...]


def ssd_mixer(z, xbc, dt, buf, h0, cw, cb, alog, dlane, gn, sel, *, nb, seq, valid):
    q = SSD_CHUNK
    nc = seq // q
    row = lambda w: pl.BlockSpec((q, w), lambda b, c: (b * nc + c, 0))
    return pl.pallas_call(
        functools.partial(_ssd_kernel, nc=nc, valid=valid),
        out_shape=(jax.ShapeDtypeStruct((nb * seq, SSD_INNER), BF16),
                   jax.ShapeDtypeStruct((nb, SSD_CONV - 1, SSD_CONV_DIM), F32),
                   jax.ShapeDtypeStruct((nb, SSD_INNER, SSD_STATE), F32)),
        grid=(nb, nc),
        in_specs=[row(SSD_INNER), row(SSD_CONV_DIM), row(LANES),
                  pl.BlockSpec((1, SSD_CONV - 1, SSD_CONV_DIM), lambda b, c: (b, 0, 0)),
                  pl.BlockSpec((1, SSD_INNER, SSD_STATE), lambda b, c: (b, 0, 0)),
                  _resident(cw.shape), _resident(cb.shape), _resident(alog.shape), _resident(dlane.shape),
                  _resident(gn.shape), _resident(sel.shape)],
        out_specs=(row(SSD_INNER),
                   pl.BlockSpec((1, SSD_CONV - 1, SSD_CONV_DIM), lambda b, c: (b, 0, 0)),
                   pl.BlockSpec((1, SSD_INNER, SSD_STATE), lambda b, c: (b, 0, 0))),
        scratch_shapes=[pltpu.VMEM((q + 8, SSD_CONV_DIM), F32),
                        pltpu.VMEM((SSD_INNER, SSD_STATE), F32),
                        pltpu.VMEM((q, SSD_INNER), F32)],
        compiler_params=_params("parallel", "arbitrary"),
        name="ssd_mixer",
    )(z, xbc, dt, buf, h0, cw, cb, alog, dlane, gn, sel)


def _cc_kernel(u_ref, buf_ref, w_ref, b_ref, g_ref, bb_ref, y_ref, nbuf_ref, ext_ref, sh_ref, *, tt, nt):
    j = pl.program_id(1)
    hist = CC_WIDTH - 1
    pad = 32
    sub = 8
    rb = min(tt, 32)

    @pl.when(j == 0)
    def _():
        ext_ref[pl.ds(pad - hist, hist), :] = buf_ref[0]

    ext_ref[pl.ds(pad, tt), :] = u_ref[...]
    for s in range(1, sub):
        sh_ref[s - 1] = ext_ref[pl.ds(s, pad + tt - sub), :]
    for r0 in range(0, tt, rb):
        acc = jnp.zeros((rb, CC_CH), F32) + b_ref[...]
        for k in range(CC_WIDTH):
            off = pad - hist + k
            base, s = off - off % sub + r0, off % sub
            rows = ext_ref[pl.ds(base, rb), :] if s == 0 else sh_ref[s - 1, pl.ds(base, rb), :]
            acc = acc + rows * w_ref[pl.ds(k, 1), :]
        mu = jnp.mean(acc, axis=-1, keepdims=True)
        xc = acc - mu
        v = xc * lax.rsqrt(jnp.mean(xc * xc, axis=-1, keepdims=True) + EPS) * g_ref[...] + bb_ref[...]
        y_ref[pl.ds(r0, rb), :] = _silu(v).astype(BF16)

    @pl.when(j == nt - 1)
    def _():
        nbuf_ref[0] = ext_ref[pl.ds(pad + tt - hist, hist), :]

    if nt > 1:
        ext_ref[pl.ds(pad - hist, hist), :] = ext_ref[pl.ds(pad + tt - hist, hist), :]


def cc_mixer(u, buf, w, b, g, bb, *, nb, seq):
    tt = min(seq, 128)
    nt = seq // tt
    assert nt == 1 or tt >= CC_WIDTH - 1
    return pl.pallas_call(
        functools.partial(_cc_kernel, tt=tt, nt=nt),
        out_shape=(jax.ShapeDtypeStruct((nb * seq, CC_CH), BF16),
                   jax.ShapeDtypeStruct((nb, CC_WIDTH - 1, CC_CH), F32)),
        grid=(nb, nt),
        in_specs=[pl.BlockSpec((tt, CC_CH), lambda i, j: (i * nt + j, 0)),
                  pl.BlockSpec((1, CC_WIDTH - 1, CC_CH), lambda i, j: (i, 0, 0)),
                  _resident(w.shape), _resident(b.shape), _resident(g.shape), _resident(bb.shape)],
        out_specs=(pl.BlockSpec((tt, CC_CH), lambda i, j: (i * nt + j, 0)),
                   pl.BlockSpec((1, CC_WIDTH - 1, CC_CH), lambda i, j: (i, 0, 0))),
        scratch_shapes=[pltpu.VMEM((32 + tt, CC_CH), F32), pltpu.VMEM((7, 24 + tt, CC_CH), F32)],
        compiler_params=_params("parallel", "arbitrary"),
        name="cc_mixer",
    )(u, buf, w, b, g, bb)


def _fox_prompt_kernel(q_ref, kt_ref, vt_ref, ct_ref, o_ref, k16_ref, v16_ref, *, tq, nq, npair):
    hg = pl.program_id(1)
    i = pl.program_id(2)

    @pl.when(i == 0)
    def _():
        k16_ref[...] = kt_ref[0].astype(BF16)
        v16_ref[...] = vt_ref[0].astype(BF16)

    def attend(ii):
        w = (ii + 1) * tq
        lane = lax.broadcasted_iota(jnp.int32, (tq, LANES), 1)
        keep = (lax.broadcasted_iota(jnp.int32, (tq, tq), 1) <= lax.broadcasted_iota(jnp.int32, (tq, tq), 0))
        ss = []
        for pp in range(npair):
            q = q_ref[:, pp * LANES:(pp + 1) * LANES]
            k16 = k16_ref[pp * LANES:(pp + 1) * LANES, :w]
            for hh in range(2):
                qh = jnp.where((lane >= hh * FOX_HEAD_DIM) & (lane < (hh + 1) * FOX_HEAD_DIM), q, jnp.zeros_like(q))
                s = _dot(qh, k16) - ct_ref[0, pl.ds(2 * (hg * npair + pp) + hh, 1), :w]
                tail = jnp.where(keep, s[:, w - tq:], -jnp.inf)
                ss.append(tail if ii == 0 else jnp.concatenate([s[:, :w - tq], tail], axis=1))
        ps = [jnp.exp(s - jnp.max(s, axis=-1, keepdims=True)) for s in ss]
        for pp in range(npair):
            v16 = v16_ref[pp * LANES:(pp + 1) * LANES, :w]
            o0, o1 = [_dot_nt(p.astype(BF16), v16) / jnp.sum(p, axis=-1, keepdims=True)
                      for p in ps[2 * pp:2 * pp + 2]]
            o_ref[:, pp * LANES:(pp + 1) * LANES] = jnp.where(lane < FOX_HEAD_DIM, o0, o1).astype(BF16)

    for ii in range(nq):
        pl.when(i == ii)(functools.partial(attend, ii))


def fox_prompt(q, kt, vt, ct, *, nb, seq):
    tq = min(seq, 256)
    nq = seq // tq
    npair = FOX_PAIRS_PER_STEP
    wd = npair * LANES
    return pl.pallas_call(
        functools.partial(_fox_prompt_kernel, tq=tq, nq=nq, npair=npair),
        out_shape=jax.ShapeDtypeStruct((nb * seq, FOX_WIDTH), BF16),
        grid=(nb, FOX_WIDTH // wd, nq),
        in_specs=[pl.BlockSpec((tq, wd), lambda b, h, i: (b * nq + i, h)),
                  pl.BlockSpec((1, wd, seq), lambda b, h, i: (b, h, 0)),
                  pl.BlockSpec((1, wd, seq), lambda b, h, i: (b, h, 0)),
                  pl.BlockSpec((1, HEAD_ROWS, seq), lambda b, h, i: (b, 0, 0))],
        out_specs=pl.BlockSpec((tq, wd), lambda b, h, i: (b * nq + i, h)),
        scratch_shapes=[pltpu.VMEM((wd, seq), BF16), pltpu.VMEM((wd, seq), BF16)],
        compiler_params=_params("parallel", "parallel", "arbitrary"),
        name="fox_prompt",
    )(q, kt, vt, ct)


def _fox_sample_kernel(pt_ref, qbd_ref, knew_ref, vnew_ref, lfnew_ref, *refs, layer, npg, nsteps, tnew):
    del pt_ref, layer
    k_refs = refs[:npg]
    v_refs = refs[npg:2 * npg]
    f_refs = refs[2 * npg:3 * npg]
    o_ref = refs[3 * npg]
    m_ref, l_ref, acc_ref, carry_ref, kcat_ref, vcat_ref = refs[3 * npg + 1:]
    j = pl.program_id(1)
    rows = 8 * HEAD_ROWS

    @pl.when(j == 0)
    def _():
        m_ref[...] = jnp.full(m_ref.shape, -1e30, F32)
        l_ref[...] = jnp.zeros(l_ref.shape, F32)
        acc_ref[...] = jnp.zeros(acc_ref.shape, F32)
        carry_ref[...] = jnp.zeros(carry_ref.shape, F32)

    qbd = qbd_ref[0]
    lane = lax.broadcasted_iota(jnp.int32, (HEAD_ROWS, LANES), 1)

    def prefix(lf):
        c = lf
        sh = 1
        while sh < LANES:
            c = c + jnp.where(lane >= sh, pltpu.roll(c, sh, axis=1), 0.0)
            sh *= 2
        return c

    def attend(kt16, vt16, lfs, keep):
        s_all = _dot(qbd, kt16)
        carry = carry_ref[...]
        ss = []
        for r, lf in enumerate(lfs):
            cum = prefix(lf) + carry
            carry = carry + jnp.sum(lf, axis=-1, keepdims=True)
            s = s_all[:, r * LANES:(r + 1) * LANES] - jnp.tile(cum, (rows // HEAD_ROWS, 1))
            if keep is not None:
                s = jnp.where(keep, s, -jnp.inf)
            ss.append(s)
        carry_ref[...] = carry
        m_cur = ss[0]
        for s in ss[1:]:
            m_cur = jnp.maximum(m_cur, s)
        m_prev = m_ref[...]
        m_new = jnp.maximum(m_prev, jnp.max(m_cur, axis=-1, keepdims=True))
        alpha = jnp.exp(m_prev - m_new)
        ps = [jnp.exp(s - m_new) for s in ss]
        p_sum = ps[0]
        for p in ps[1:]:
            p_sum = p_sum + p
        l_ref[...] = alpha * l_ref[...] + jnp.sum(p_sum, axis=-1, keepdims=True)
        m_ref[...] = m_new
        p16 = jnp.concatenate([p.astype(BF16) for p in ps], axis=1) if len(ps) > 1 else ps[0].astype(BF16)
        pv = _dot_nt(vt16, p16)
        acc_ref[...] = acc_ref[...] * alpha.T[0:1, :] + pv

    zpad = jnp.zeros((HEAD_ROWS - FOX_HEADS, LANES), F32)
    for r in range(npg):
        kcat_ref[:, r * LANES:(r + 1) * LANES] = k_refs[r][0, 0].astype(BF16)
        vcat_ref[:, r * LANES:(r + 1) * LANES] = v_refs[r][0, 0].astype(BF16)
    attend(kcat_ref[...], vcat_ref[...],
           [jnp.concatenate([f_refs[r][0], zpad], axis=0) for r in range(npg)], None)

    @pl.when(j == nsteps - 1)
    def _():
        srow = lax.broadcasted_iota(jnp.int32, (rows, LANES), 0)
        scol = lax.broadcasted_iota(jnp.int32, (rows, LANES), 1)
        keep = (scol <= srow // HEAD_ROWS) & (scol < tnew)
        lf_new = jnp.where(lane < tnew, lfnew_ref[0], 0.0)
        attend(knew_ref[0].astype(BF16), vnew_ref[0].astype(BF16), [lf_new], keep)
        o_t = acc_ref[...] * (1.0 / l_ref[...]).T[0:1, :]
        o = o_t.T
        orow = lax.broadcasted_iota(jnp.int32, (rows, FOX_WIDTH), 0)
        ocol = lax.broadcasted_iota(jnp.int32, (rows, FOX_WIDTH), 1)
        o = jnp.where((orow % HEAD_ROWS) == (ocol // FOX_HEAD_DIM), o, 0.0)
        pick = (lax.broadcasted_iota(jnp.int32, (8, rows), 1) // HEAD_ROWS
                == lax.broadcasted_iota(jnp.int32, (8, rows), 0)).astype(F32)
        o_ref[0] = _dot_hi(pick, o).astype(BF16)


def fox_sample(qbd, knew, vnew, lfnew, kcache, vcache, fcache, page_table, *, layer, tnew):
    nb, n_pages = page_table.shape
    npg = min(PAGES_PER_STEP, n_pages)
    nsteps = n_pages // npg
    assert tnew <= 8

    def page_spec(r, shape, kind):
        if kind == "kv":
            return pl.BlockSpec((1, 1, FOX_WIDTH, PAGE_SIZE), lambda b, j, pt: (layer, pt[b, j * npg + r], 0, 0))
        return pl.BlockSpec((1, FOX_HEADS, PAGE_SIZE), lambda b, j, pt: (layer, 0, pt[b, j * npg + r]))

    per_seq = lambda shape: pl.BlockSpec((1,) + shape, lambda b, j, pt: (b, 0, 0))
    in_specs = [per_seq((8 * HEAD_ROWS, FOX_WIDTH)), per_seq((FOX_WIDTH, LANES)), per_seq((FOX_WIDTH, LANES)),
                per_seq((HEAD_ROWS, LANES))]
    in_specs += [page_spec(r, None, "kv") for r in range(npg)]
    in_specs += [page_spec(r, None, "kv") for r in range(npg)]
    in_specs += [page_spec(r, None, "f") for r in range(npg)]
    grid_spec = pltpu.PrefetchScalarGridSpec(
        num_scalar_prefetch=1,
        grid=(nb, nsteps),
        in_specs=in_specs,
        out_specs=pl.BlockSpec((1, 8, FOX_WIDTH), lambda b, j, pt: (b, 0, 0)),
        scratch_shapes=[pltpu.VMEM((8 * HEAD_ROWS, LANES), F32), pltpu.VMEM((8 * HEAD_ROWS, LANES), F32),
                        pltpu.VMEM((FOX_WIDTH, LANES), F32), pltpu.VMEM((HEAD_ROWS, LANES), F32),
                        pltpu.VMEM((FOX_WIDTH, npg * PAGE_SIZE), BF16),
                        pltpu.VMEM((FOX_WIDTH, npg * PAGE_SIZE), BF16)],
    )
    return pl.pallas_call(
        functools.partial(_fox_sample_kernel, layer=layer, npg=npg, nsteps=nsteps, tnew=tnew),
        out_shape=jax.ShapeDtypeStruct((nb, 8, FOX_WIDTH), BF16),
        grid_spec=grid_spec,
        compiler_params=_params("parallel", "arbitrary"),
        name="fox_sample",
    )(page_table, qbd, knew, vnew, lfnew, *([kcache] * npg), *([vcache] * npg), *([fcache] * npg))


def _outproj_kernel(ys_ref, yc_ref, yf_ref, w_ref, x_ref, g_ref, xo_ref, no_ref):
    a = SSD_INNER
    b = SSD_INNER + CC_CH
    acc = _dot(ys_ref[...], w_ref[0, 0:a, :])
    acc = acc + _dot(yc_ref[...], w_ref[0, a:b, :])
    acc = acc + _dot(yf_ref[...], w_ref[0, b:, :])
    xn = x_ref[...] + acc
    xo_ref[...] = xn
    no_ref[...] = _rms_rows(xn, g_ref[...]).astype(BF16)


def out_proj(ys, yc, yf, w, x, g, *, layer):
    m, d = x.shape
    tm = min(m, 512)
    row = lambda wd: pl.BlockSpec((tm, wd), lambda i: (i, 0))
    return pl.pallas_call(
        _outproj_kernel,
        out_shape=(jax.ShapeDtypeStruct((m, d), F32), jax.ShapeDtypeStruct((m, d), BF16)),
        grid=(m // tm,),
        in_specs=[row(SSD_INNER), row(CC_CH), row(FOX_WIDTH), _layer_block(w.shape[1:], layer), row(d),
                  _resident(g.shape)],
        out_specs=(row(d), row(d)),
        compiler_params=_params("parallel"),
        name="out_proj",
    )(ys, yc, yf, w, x, g)


def _memkv_kernel(m_ref, gm_ref, wk_ref, wv_ref, gk_ref, k_ref, v_ref):
    mn = _rms_rows(m_ref[...], gm_ref[...]).astype(BF16)
    k = _dot(mn, wk_ref[0])
    for h in range(XA_HEADS):
        lo, hi = h * XA_HEAD_DIM, (h + 1) * XA_HEAD_DIM
        k_ref[:, lo:hi] = _rms_rows(k[:, lo:hi], gk_ref[...])
    v_ref[...] = _dot(mn, wv_ref[0])


def mem_kv(mem, gm, wk, wv, gk, *, layer):
    m, d = mem.shape
    tm = min(m, 256)
    row = lambda wd: pl.BlockSpec((tm, wd), lambda i: (i, 0))
    return pl.pallas_call(
        _memkv_kernel,
        out_shape=(jax.ShapeDtypeStruct((m, XA_WIDTH), F32), jax.ShapeDtypeStruct((m, XA_WIDTH), F32)),
        grid=(m // tm,),
        in_specs=[row(d), _resident(gm.shape), _layer_block(wk.shape[1:], layer), _layer_block(wv.shape[1:], layer),
                  _resident(gk.shape)],
        out_specs=(row(XA_WIDTH), row(XA_WIDTH)),
        compiler_params=_params("parallel"),
        name="mem_kv",
    )(mem, gm, wk, wv, gk)


def _xattn_kernel(n_ref, x_ref, wq_ref, qg_ref, mk_ref, mv_ref, wo_ref, g_ref, xo_ref, no_ref):
    q = _dot(n_ref[...], wq_ref[0])
    mk = mk_ref[0].astype(BF16)
    mv = mv_ref[0].astype(BF16)
    outs = []
    for h in range(XA_HEADS):
        lo, hi = h * XA_HEAD_DIM, (h + 1) * XA_HEAD_DIM
        qh = _rms_rows(q[:, lo:hi], qg_ref[...]).astype(BF16)
        s = _dot_nt(qh, mk[:, lo:hi]) * (XA_HEAD_DIM ** -0.5)
        s = s - jnp.max(s, axis=-1, keepdims=True)
        e = jnp.exp(s)
        p = e / jnp.sum(e, axis=-1, keepdims=True)
        outs.append(_dot(p.astype(BF16), mv[:, lo:hi]))
    o = jnp.concatenate(outs, axis=-1).astype(BF16)
    xn = x_ref[...] + _dot(o, wo_ref[0])
    xo_ref[...] = xn
    no_ref[...] = _rms_rows(xn, g_ref[...]).astype(BF16)


def cross_attn(n, x, wq, qg, mk, mv, wo, g, *, layer, seq):
    m, d = x.shape
    tm = min(seq, 512)
    per = seq // tm
    mem = mk.shape[1]
    row = lambda wd: pl.BlockSpec((tm, wd), lambda i: (i, 0))
    kv = pl.BlockSpec((1, mem, XA_WIDTH), lambda i: (i // per, 0, 0))
    return pl.pallas_call(
        _xattn_kernel,
        out_shape=(jax.ShapeDtypeStruct((m, d), F32), jax.ShapeDtypeStruct((m, d), BF16)),
        grid=(m // tm,),
        in_specs=[row(d), row(d), _layer_block(wq.shape[1:], layer), _resident(qg.shape), kv, kv,
                  _layer_block(wo.shape[1:], layer), _resident(g.shape)],
        out_specs=(row(d), row(d)),
        compiler_params=_params("parallel"),
        name="cross_attn",
    )(n, x, wq, qg, mk, mv, wo, g)


def _mix_xattn_kernel(ys_ref, yc_ref, yf_ref, w_ref, x_ref, gx_ref, wq_ref, qg_ref, mk_ref, mv_ref, wo_ref, g_ref,
                      xo_ref, no_ref):
    a = SSD_INNER
    b = SSD_INNER + CC_CH
    acc = _dot(ys_ref[...], w_ref[0, 0:a, :])
    acc = acc + _dot(yc_ref[...], w_ref[0, a:b, :])
    acc = acc + _dot(yf_ref[...], w_ref[0, b:, :])
    x1 = x_ref[...] + acc
    n1 = _rms_rows(x1, gx_ref[...]).astype(BF16)
    q = _dot(n1, wq_ref[0])
    mk = mk_ref[0].astype(BF16)
    mv = mv_ref[0].astype(BF16)
    outs = []
    for h in range(XA_HEADS):
        lo, hi = h * XA_HEAD_DIM, (h + 1) * XA_HEAD_DIM
        qh = _rms_rows(q[:, lo:hi], qg_ref[...]).astype(BF16)
        s = _dot_nt(qh, mk[:, lo:hi]) * (XA_HEAD_DIM ** -0.5)
        s = s - jnp.max(s, axis=-1, keepdims=True)
        e = jnp.exp(s)
        p = e / jnp.sum(e, axis=-1, keepdims=True)
        outs.append(_dot(p.astype(BF16), mv[:, lo:hi]))
    o = jnp.concatenate(outs, axis=-1).astype(BF16)
    x2 = x1 + _dot(o, wo_ref[0])
    xo_ref[...] = x2
    no_ref[...] = _rms_rows(x2, g_ref[...]).astype(BF16)


def mix_xattn(ys, yc, yf, w, x, gx, wq, qg, mk, mv, wo, g, *, layer, seq):
    m, d = x.shape
    tm = min(seq, 512)
    per = seq // tm
    mem = mk.shape[1]
    row = lambda wd: pl.BlockSpec((tm, wd), lambda i: (i, 0))
    kv = pl.BlockSpec((1, mem, XA_WIDTH), lambda i: (i // per, 0, 0))
    return pl.pallas_call(
        _mix_xattn_kernel,
        out_shape=(jax.ShapeDtypeStruct((m, d), F32), jax.ShapeDtypeStruct((m, d), BF16)),
        grid=(m // tm,),
        in_specs=[row(SSD_INNER), row(CC_CH), row(FOX_WIDTH), _layer_block(w.shape[1:], layer), row(d),
                  _resident(gx.shape), _layer_block(wq.shape[1:], layer), _resident(qg.shape), kv, kv,
                  _layer_block(wo.shape[1:], layer), _resident(g.shape)],
        out_specs=(row(d), row(d)),
        compiler_params=_params("parallel"),
        name="mix_xattn",
    )(ys, yc, yf, w, x, gx, wq, qg, mk, mv, wo, g)


def _ffn_kernel(n_ref, x_ref, wg_ref, wu_ref, wd_ref, g_ref, xo_ref, no_ref, acc_ref, *, nf):
    f = pl.program_id(1)

    @pl.when(f == 0)
    def _():
        acc_ref[...] = jnp.zeros_like(acc_ref)

    n = n_ref[...]
    a = (_silu(_dot(n, wg_ref[0])) * _dot(n, wu_ref[0])).astype(BF16)
    acc_ref[...] += _dot(a, wd_ref[0])

    @pl.when(f == nf - 1)
    def _():
        xn = x_ref[...] + acc_ref[...]
        xo_ref[...] = xn
        no_ref[...] = _rms_rows(xn, g_ref[...]).astype(BF16)


def ffn(n, x, wg, wu, wd, g, *, layer):
    m, d = x.shape
    tm = min(m, 512)
    tf = FFN_TILE
    nf = wg.shape[2] // tf
    row = lambda wdt: pl.BlockSpec((tm, wdt), lambda i, f: (i, 0))
    return pl.pallas_call(
        functools.partial(_ffn_kernel, nf=nf),
        out_shape=(jax.ShapeDtypeStruct((m, d), F32), jax.ShapeDtypeStruct((m, d), BF16)),
        grid=(m // tm, nf),
        in_specs=[row(d), row(d),
                  pl.BlockSpec((1, d, tf), lambda i, f: (layer, 0, f)),
                  pl.BlockSpec((1, d, tf), lambda i, f: (layer, 0, f)),
                  pl.BlockSpec((1, tf, d), lambda i, f: (layer, f, 0)), _resident(g.shape)],
        out_specs=(row(d), row(d)),
        scratch_shapes=[pltpu.VMEM((tm, d), F32)],
        compiler_params=_params("parallel", "arbitrary"),
        name="ffn",
    )(n, x, wg, wu, wd, g)


def _head_selector():
    r = jnp.arange(LANES)[:, None]
    c = jnp.arange(FOX_WIDTH)[None, :]
    return (r == c // FOX_HEAD_DIM).astype(BF16)


def _pad_lanes(v, width=LANES):
    return jnp.pad(v, (0, width - v.shape[0]))[None, :]


def kernel(x_prompt, x_sample, mem_prompt, cache_fox_k, cache_fox_v, cache_fox_logf, page_table, state_ssd, state_ssd_conv, state_cc_conv, cache_mem_k, cache_mem_v, norm_mix, w_in, ssd_conv_w, ssd_conv_b, ssd_dt_bias, ssd_a_log, ssd_d, ssd_norm, cc_dw_w, cc_dw_b, cc_ln_g, cc_ln_b, fox_q_norm, fox_k_norm, fox_fg_bias, w_out, norm_xa, norm_mem, xa_wq, xa_wk, xa_wv, xa_q_norm, xa_k_norm, xa_wo, norm_ffn, ffn_wg, ffn_wu, ffn_wd):
    depth = norm_mix.shape[0]
    bp, lp, d = x_prompt.shape
    bs, ls, _ = x_sample.shape
    mp, ms = bp * lp, bs * ls
    n_pool = cache_fox_k.shape[1]
    mem_len = mem_prompt.shape[1]
    q = SSD_CHUNK

    sel = _head_selector()
    selt = sel.T
    kcache = jnp.transpose(cache_fox_k, (0, 1, 3, 4, 2)).reshape(depth, n_pool, FOX_WIDTH, PAGE_SIZE)
    vcache = jnp.transpose(cache_fox_v, (0, 1, 3, 4, 2)).reshape(depth, n_pool, FOX_WIDTH, PAGE_SIZE)
    fcache = jnp.transpose(cache_fox_logf, (0, 3, 1, 2)).reshape(depth, FOX_HEADS, n_pool * PAGE_SIZE)
    w_in_t = jnp.transpose(w_in, (0, 2, 1))
    w_a = w_in_t.astype(BF16)
    w_b = w_a[:, _OFF_GLU:]
    w_o = w_out.astype(BF16)
    wxq, wxk, wxv, wxo = (w.astype(BF16) for w in (xa_wq, xa_wk, xa_wv, xa_wo))
    wg, wu, wd = ffn_wg.astype(BF16), ffn_wu.astype(BF16), ffn_wd.astype(BF16)

    xp = x_prompt.reshape(mp, d)
    xs = x_sample.reshape(ms, d)
    mem = mem_prompt.reshape(bp * mem_len, d)
    np_ = rmsnorm_bf16(xp, norm_mix[0][None, :])
    ns_ = rmsnorm_bf16(xs, norm_mix[0][None, :])

    zeros_ssd_buf = jnp.zeros((bp, SSD_CONV - 1, SSD_CONV_DIM), F32)
    zeros_ssd_h = jnp.zeros((bp, SSD_INNER, SSD_STATE), F32)
    zeros_cc_buf = jnp.zeros((bp, CC_WIDTH - 1, CC_CH), F32)

    outs = [[] for _ in range(14)]
    for i in range(depth):
        bd = _pad_lanes(ssd_dt_bias[i])
        qg = (jnp.tile(fox_q_norm[i], FOX_HEADS) * (FOX_HEAD_DIM ** -0.5))[None, :]
        gk = jnp.broadcast_to(jnp.tile(fox_k_norm[i], FOX_HEADS)[:, None], (FOX_WIDTH, LANES))
        bfg = jnp.broadcast_to(jnp.pad(fox_fg_bias[i], (0, HEAD_ROWS - FOX_HEADS))[:, None], (HEAD_ROWS, LANES))
        alog = _pad_lanes(ssd_a_log[i])
        dlane = jnp.repeat(ssd_d[i], SSD_HEAD_DIM)[None, :]
        gn = ssd_norm[i][None, :]
        g_next = norm_mix[i + 1][None, :] if i + 1 < depth else jnp.ones((1, d), F32)
        conv_args = (ssd_conv_w[i], ssd_conv_b[i][None, :], alog, dlane, gn, sel)
        cc_args = (cc_dw_w[i], cc_dw_b[i][None, :], cc_ln_g[i][None, :], cc_ln_b[i][None, :])

        mk, mv = mem_kv(mem, norm_mem[i][None, :], wxk, wxv, xa_k_norm[i][None, :], layer=i)
        z, xbc, dt = proj_ssd(np_, w_a, bd, layer=i)
        ucc, qn = proj_ccq(np_, w_b, qg, sel, selt, layer=i)
        kt, vt, lft, cumt = proj_t(np_, w_b, gk, bfg, layer=i, nb=bp, seq=lp, seg=lp)
        y_ssd, ssd_buf, ssd_h = ssd_mixer(z, xbc, dt, zeros_ssd_buf, zeros_ssd_h, *conv_args,
                                          nb=bp, seq=lp, valid=lp)
        y_cc, cc_buf = cc_mixer(ucc, zeros_cc_buf, *cc_args, nb=bp, seq=lp)
        y_fox = fox_prompt(qn, kt, vt, cumt, nb=bp, seq=lp)
        xp, nff = mix_xattn(y_ssd, y_cc, y_fox, w_o, xp, norm_xa[i][None, :], wxq, xa_q_norm[i][None, :],
                            mk.reshape(bp, mem_len, XA_WIDTH), mv.reshape(bp, mem_len, XA_WIDTH), wxo,
                            norm_ffn[i][None, :], layer=i, seq=lp)
        xp, np_ = ffn(nff, xp, wg, wu, wd, g_next, layer=i)
        outs[0].append(jnp.transpose(kt.reshape(bp, FOX_HEADS, FOX_HEAD_DIM, lp), (0, 3, 1, 2)))
        outs[1].append(jnp.transpose(vt.reshape(bp, FOX_HEADS, FOX_HEAD_DIM, lp), (0, 3, 1, 2)))
        outs[2].append(jnp.transpose(lft[:, :FOX_HEADS, :], (0, 2, 1)))
        outs[3].append(ssd_h.reshape(bp, SSD_HEADS, SSD_HEAD_DIM, SSD_STATE))
        outs[4].append(ssd_buf)
        outs[5].append(cc_buf)
        outs[6].append(mk.reshape(bp, mem_len, XA_HEADS, XA_HEAD_DIM))
        outs[7].append(mv.reshape(bp, mem_len, XA_HEADS, XA_HEAD_DIM))

        z, xbc, dt = proj_ssd(ns_, w_a, bd, layer=i)
        ucc, qn = proj_ccq(ns_, w_b, qg, sel, selt, layer=i)
        kt, vt, lft, _ = proj_t(ns_, w_b, gk, bfg, layer=i, nb=1, seq=ms, seg=ls)
        pad_tok = lambda a: jnp.pad(a.reshape(bs, ls, a.shape[-1]), ((0, 0), (0, q - ls), (0, 0))).reshape(bs * q, a.shape[-1])
        y_ssd, ssd_buf, ssd_h = ssd_mixer(pad_tok(z), pad_tok(xbc), pad_tok(dt), state_ssd_conv[i],
                                          state_ssd[i].reshape(bs, SSD_INNER, SSD_STATE), *conv_args,
                                          nb=bs, seq=q, valid=ls)
        y_ssd = y_ssd.reshape(bs, q, SSD_INNER)[:, :ls].reshape(ms, SSD_INNER)
        y_cc, cc_buf = cc_mixer(ucc, state_cc_conv[i], *cc_args, nb=bs, seq=ls)
        q4 = qn.reshape(bs, ls, FOX_HEADS, 1, FOX_HEAD_DIM)
        eye = jnp.eye(FOX_HEADS, dtype=BF16)[None, None, :, :, None]
        qbd = (q4 * eye).reshape(bs, ls, FOX_HEADS, FOX_WIDTH)
        qbd = jnp.pad(qbd, ((0, 0), (0, 8 - ls), (0, HEAD_ROWS - FOX_HEADS), (0, 0))).reshape(bs, 8 * HEAD_ROWS, FOX_WIDTH)
        to_seq = lambda a: jnp.pad(jnp.transpose(a[0].reshape(a.shape[1], bs, ls), (1, 0, 2)),
                                   ((0, 0), (0, 0), (0, LANES - ls)))
        y_fox = fox_sample(qbd, to_seq(kt), to_seq(vt), to_seq(lft), kcache, vcache, fcache, page_table,
                           layer=i, tnew=ls)
        y_fox = y_fox[:, :ls].reshape(ms, FOX_WIDTH)
        xs, nxa = out_proj(y_ssd, y_cc, y_fox, w_o, xs, norm_xa[i][None, :], layer=i)
        xs, nff = cross_attn(nxa, xs, wxq, xa_q_norm[i][None, :], cache_mem_k[i].reshape(bs, mem_len, XA_WIDTH),
                             cache_mem_v[i].reshape(bs, mem_len, XA_WIDTH), wxo, norm_ffn[i][None, :], layer=i,
                             seq=ls)
        xs, ns_ = ffn(nff, xs, wg, wu, wd, g_next, layer=i)
        tok = lambda a, w: jnp.transpose(a[0], (1, 0)).reshape(bs, ls, w)
        outs[8].append(tok(kt, FOX_WIDTH).reshape(bs, ls, FOX_HEADS, FOX_HEAD_DIM))
        outs[9].append(tok(vt, FOX_WIDTH).reshape(bs, ls, FOX_HEADS, FOX_HEAD_DIM))
        outs[10].append(tok(lft, HEAD_ROWS)[:, :, :FOX_HEADS])
        outs[11].append(ssd_h.reshape(bs, SSD_HEADS, SSD_HEAD_DIM, SSD_STATE))
        outs[12].append(ssd_buf)
        outs[13].append(cc_buf)

    return (xp.reshape(bp, lp, d), xs.reshape(bs, ls, d)) + tuple(jnp.stack(o) for o in outs)
```

```python
import functools
import math

import jax
import jax.numpy as jnp
from jax import lax
from jax.experimental import pallas as pl
from jax.experimental.pallas import tpu as pltpu

F32 = jnp.float32
BF16 = jnp.bfloat16
HI = lax.Precision.HIGHEST

D_MODEL = 2048
SSD_HEADS = 12
SSD_HEAD_DIM = 64
SSD_INNER = SSD_HEADS * SSD_HEAD_DIM
SSD_GROUPS = 2
SSD_STATE = 128
SSD_CONV = 4
SSD_CHUNK = 128
SSD_CONV_DIM = SSD_INNER + 2 * SSD_GROUPS * SSD_STATE
CC_CH = 512
CC_WIDTH = 31
FOX_HEADS = 12
FOX_HEAD_DIM = 64
FOX_WIDTH = FOX_HEADS * FOX_HEAD_DIM
PAGE_SIZE = 128
XA_HEADS = 4
XA_HEAD_DIM = 128
XA_WIDTH = XA_HEADS * XA_HEAD_DIM
EPS = 1e-6

LANES = 128
HEAD_ROWS = 16
VMEM_LIMIT_BYTES = 56 * 1024 * 1024
PAGES_PER_STEP = 16
FOX_PAIRS_PER_STEP = 2
SSD_SEQS_PER_STEP = 4

_OFF_Z = 0
_OFF_XBC = _OFF_Z + SSD_INNER
_OFF_DT = _OFF_XBC + SSD_CONV_DIM
_OFF_GLU = _OFF_DT + SSD_HEADS
_OFF_Q = _OFF_GLU + 2 * CC_CH
_OFF_K = _OFF_Q + FOX_WIDTH
_OFF_V = _OFF_K + FOX_WIDTH
_OFF_FG = _OFF_V + FOX_WIDTH

_A_ROWS = _OFF_DT + LANES
_B_GLU = 0
_B_Q = _B_GLU + 2 * CC_CH
_B_K = _B_Q + FOX_WIDTH
_B_V = _B_K + FOX_WIDTH
_B_FG = _B_V + FOX_WIDTH
_B_ROWS = _B_FG + FOX_HEADS
FFN_TILE = 512


def _params(*sem):
    return pltpu.CompilerParams(dimension_semantics=sem, vmem_limit_bytes=VMEM_LIMIT_BYTES)


def _resident(shape):
    nd = len(shape)
    return pl.BlockSpec(shape, lambda *_: (0,) * nd, pipeline_mode=pl.Buffered(1))


def _layer_block(shape, layer, block=0):
    return pl.BlockSpec((1,) + tuple(shape), lambda *_: (layer, block, 0), pipeline_mode=pl.Buffered(1))


def _rms_rows(x, g):
    ms = jnp.mean(x * x, axis=-1, keepdims=True)
    return x * lax.rsqrt(ms + EPS) * g


def _softplus(x):
    return jnp.maximum(x, 0.0) + jnp.log1p(jnp.exp(-jnp.abs(x)))


def _silu(x):
    return x * jax.nn.sigmoid(x)


def _dot(a, b):
    return jnp.dot(a, b, preferred_element_type=F32)


def _dot_hi(a, b):
    return jnp.dot(a, b, precision=HI, preferred_element_type=F32)


def _dot_split(a, b, passes=3):
    a_exact = a.dtype == BF16
    rest = b if a_exact else a
    out = None
    for _ in range(passes):
        hi = rest.astype(BF16)
        term = _dot(a, hi) if a_exact else _dot(hi, b)
        out = term if out is None else out + term
        rest = rest - hi.astype(F32)
    return out


def _dot_nt(a, b):
    return lax.dot_general(a, b, (((1,), (1,)), ((), ())), preferred_element_type=F32)


def _dot_tn(a, b):
    return lax.dot_general(a, b, (((0,), (0,)), ((), ())), preferred_element_type=F32)


def _norm_kernel(x_ref, g_ref, o_ref):
    o_ref[...] = _rms_rows(x_ref[...], g_ref[...]).astype(BF16)


def rmsnorm_bf16(x, g):
    m, d = x.shape
    tm = min(m, 512)
    return pl.pallas_call(
        _norm_kernel,
        out_shape=jax.ShapeDtypeStruct((m, d), BF16),
        grid=(m // tm,),
        in_specs=[pl.BlockSpec((tm, d), lambda i: (i, 0)), pl.BlockSpec((1, d), lambda i: (0, 0))],
        out_specs=pl.BlockSpec((tm, d), lambda i: (i, 0)),
        compiler_params=_params("parallel"),
        name="rmsnorm_bf16",
    )(x, g)


def _proj_ssd_kernel(n_ref, w_ref, bd_ref, z_ref, xbc_ref, dt_ref):
    n = n_ref[...]
    zx = _dot_nt(n, w_ref[0, _OFF_Z:_OFF_DT, :])
    z_ref[...] = zx[:, :SSD_INNER]
    xbc_ref[...] = zx[:, SSD_INNER:]
    raw = _dot_nt(n, w_ref[0, _OFF_DT:_A_ROWS, :])
    lane = lax.broadcasted_iota(jnp.int32, raw.shape, 1)
    dt_ref[...] = _softplus(jnp.where(lane < SSD_HEADS, raw, 0.0) + bd_ref[...])


def proj_ssd(n, w_a, bd, *, layer):
    m, d = n.shape
    tm = min(m, 512)
    row = lambda w: pl.BlockSpec((tm, w), lambda i: (i, 0))
    return pl.pallas_call(
        _proj_ssd_kernel,
        out_shape=(jax.ShapeDtypeStruct((m, SSD_INNER), F32),
                   jax.ShapeDtypeStruct((m, SSD_CONV_DIM), F32),
                   jax.ShapeDtypeStruct((m, LANES), F32)),
        grid=(m // tm,),
        in_specs=[row(d), _layer_block((_A_ROWS, d), layer), _resident(bd.shape)],
        out_specs=(row(SSD_INNER), row(SSD_CONV_DIM), row(LANES)),
        compiler_params=_params("parallel"),
        name="proj_ssd",
    )(n, w_a, bd)


def _proj_ccq_kernel(n_ref, w_ref, qg_ref, sel_ref, selt_ref, u_ref, q_ref):
    n = n_ref[...]
    glu = _dot_nt(n, w_ref[0, _B_GLU:_B_Q, :])
    u_ref[...] = glu[:, :CC_CH] * jax.nn.sigmoid(glu[:, CC_CH:])
    q = _dot_nt(n, w_ref[0, _B_Q:_B_K, :])
    ms = _dot_split(q * q, selt_ref[...], 2) * (1.0 / FOX_HEAD_DIM)
    r = _dot_split(lax.rsqrt(ms + EPS), sel_ref[...], 2)
    q_ref[...] = (q * r * qg_ref[...]).astype(BF16)


def proj_ccq(n, w_b, qg, sel, selt, *, layer):
    m, d = n.shape
    tm = min(m, 512)
    row = lambda w: pl.BlockSpec((tm, w), lambda i: (i, 0))
    return pl.pallas_call(
        _proj_ccq_kernel,
        out_shape=(jax.ShapeDtypeStruct((m, CC_CH), F32), jax.ShapeDtypeStruct((m, FOX_WIDTH), BF16)),
        grid=(m // tm,),
        in_specs=[row(d), _layer_block((_B_K, d), layer), _resident(qg.shape),
                  _resident(sel.shape), _resident(selt.shape)],
        out_specs=(row(CC_CH), row(FOX_WIDTH)),
        compiler_params=_params("parallel"),
        name="proj_ccq",
    )(n, w_b, qg, sel, selt)


def _proj_t_kernel(n_ref, w_ref, gk_ref, bf_ref, kt_ref, vt_ref, lf_ref, cum_ref, carry_ref, *, tm, seg):
    j = pl.program_id(1)
    n = n_ref[...]
    kt = _dot_nt(w_ref[0, _B_K:_B_V, :], n)
    k3 = kt.reshape(FOX_HEADS, FOX_HEAD_DIM, tm)
    ms = jnp.mean(k3 * k3, axis=1, keepdims=True)
    k3 = k3 * lax.rsqrt(ms + EPS)
    kt_ref[0] = k3.reshape(FOX_WIDTH, tm) * jnp.tile(gk_ref[...], (1, tm // gk_ref.shape[1]))
    vt_ref[0] = _dot_nt(w_ref[0, _B_V:_B_FG, :], n)
    raw = jnp.concatenate([_dot_nt(w_ref[0, _B_FG:_B_ROWS, :], n),
                           jnp.zeros((HEAD_ROWS - FOX_HEADS, tm), F32)], axis=0)
    raw = raw + jnp.tile(bf_ref[...], (1, tm // bf_ref.shape[1]))
    lf = -_softplus(-raw)
    lf_ref[0] = lf

    @pl.when((j * tm) % seg == 0)
    def _():
        carry_ref[...] = jnp.zeros_like(carry_ref)

    s_idx = lax.broadcasted_iota(jnp.int32, (tm, tm), 0)
    t_idx = lax.broadcasted_iota(jnp.int32, (tm, tm), 1)
    upper = s_idx <= t_idx
    if seg % tm != 0:
        upper = upper & ((s_idx // seg) == (t_idx // seg))
    cum = _dot_split(lf, jnp.where(upper, 1.0, 0.0).astype(BF16))
    carry = carry_ref[...]
    cum_ref[0] = cum + jnp.tile(carry, (1, tm // LANES)) if tm >= LANES else cum + carry[:, :tm]
    carry_ref[...] = carry + _dot_split(lf, jnp.ones((tm, LANES), BF16))


def proj_t(n, w_b, gk, bf, *, layer, nb, seq, seg):
    m, d = n.shape
    tm = min(seq, 512)
    assert seg % tm == 0 or seq == tm
    nj = seq // tm
    lane = min(tm, LANES)
    gk = gk[:, :lane]
    bf = bf[:, :lane]
    big = lambda rows: pl.BlockSpec((1, rows, tm), lambda b, j: (b, 0, j))
    return pl.pallas_call(
        functools.partial(_proj_t_kernel, tm=tm, seg=seg),
        out_shape=(jax.ShapeDtypeStruct((nb, FOX_WIDTH, seq), F32),
                   jax.ShapeDtypeStruct((nb, FOX_WIDTH, seq), F32),
                   jax.ShapeDtypeStruct((nb, HEAD_ROWS, seq), F32),
                   jax.ShapeDtypeStruct((nb, HEAD_ROWS, seq), F32)),
        grid=(nb, nj),
        in_specs=[pl.BlockSpec((tm, d), lambda b, j: (b * nj + j, 0)),
                  _layer_block(w_b.shape[1:], layer), _resident(gk.shape), _resident(bf.shape)],
        out_specs=(big(FOX_WIDTH), big(FOX_WIDTH), big(HEAD_ROWS), big(HEAD_ROWS)),
        scratch_shapes=[pltpu.VMEM((HEAD_ROWS, LANES), F32)],
        compiler_params=_params("parallel", "arbitrary"),
        name="proj_t",
    )(n, w_b, gk, bf)


_SSD_PAD = 8


def _ssd_kernel(z_ref, xbc_ref, dt_ref, buf_ref, h0_ref, cw_ref, cb_ref, alog_ref, dl_ref, gn_ref, sel_ref,
                y_ref, nbuf_ref, hl_ref, ext_ref, h_ref, yg_ref, *, nc, valid, group):
    c = pl.program_id(1)

    @pl.when(c == 0)
    def _():
        for bb in range(group):
            ext_ref[bb, pl.ds(_SSD_PAD - 3, 3), :] = buf_ref[bb]
            h_ref[bb] = h0_ref[bb]

    for bb in range(group):
        _ssd_chunk(c, z_ref.at[bb], xbc_ref.at[bb], dt_ref.at[bb], cw_ref, cb_ref, alog_ref, dl_ref, gn_ref, sel_ref,
                   y_ref.at[bb], nbuf_ref.at[bb], hl_ref.at[bb], ext_ref.at[bb], h_ref.at[bb], yg_ref.at[bb],
                   nc=nc, valid=valid)


def _ssd_chunk(c, z_ref, xbc_ref, dt_ref, cw_ref, cb_ref, alog_ref, dl_ref, gn_ref, sel_ref,
               y_ref, nbuf_ref, hl_ref, ext_ref, h_ref, yg_ref, *, nc, valid):
    q = SSD_CHUNK
    pad = _SSD_PAD
    assert (valid - 1) // q == nc - 1
    t_last = (valid - 1) % q

    ext_ref[pl.ds(pad, q), :] = xbc_ref[...]
    u = jnp.zeros((q, SSD_CONV_DIM), F32) + cb_ref[...]
    for j in range(SSD_CONV):
        u = u + ext_ref[pl.ds(pad - 3 + j, q), :] * cw_ref[pl.ds(j, 1), :]
    u = _silu(u)

    nbuf_ref[...] = ext_ref[pl.ds(pad + t_last - 2, 3), :]
    ext_ref[pl.ds(pad - 3, 3), :] = ext_ref[pl.ds(pad + q - 3, 3), :]

    row = lax.broadcasted_iota(jnp.int32, (q, LANES), 0)
    col = lax.broadcasted_iota(jnp.int32, (q, LANES), 1)
    dt = jnp.where(c * q + row < valid, dt_ref[...], 0.0)
    a = -jnp.exp(alog_ref[...])
    tril = row >= col
    cum = _dot_split(jnp.where(tril, 1.0, 0.0).astype(BF16), dt * a)
    sel = sel_ref[...]
    cumw = _dot_split(cum, sel)
    dtw = _dot_split(dt, sel)
    cum_t = cum.T
    dt_t = dt.T
    lastw = cumw[q - 1:q, :]
    w_in_state = jnp.exp(lastw - cumw) * dtw
    e_cumw = jnp.exp(cumw)
    e_last = jnp.exp(cum[q - 1:q, :])
    lane_lo = col < SSD_HEAD_DIM
    row_lo = row < SSD_HEAD_DIM

    hpg = SSD_HEADS // SSD_GROUPS
    ssq = jnp.zeros((q, 1), F32)
    for g in range(SSD_GROUPS):
        bm = u[:, SSD_INNER + g * SSD_STATE:SSD_INNER + (g + 1) * SSD_STATE]
        cm = u[:, SSD_INNER + (SSD_GROUPS + g) * SSD_STATE:SSD_INNER + (SSD_GROUPS + g + 1) * SSD_STATE]
        bm16 = bm.astype(BF16)
        cm16 = cm.astype(BF16)
        cb = _dot_nt(cm16, bm16)
        for pp in range(hpg // 2):
            p = g * (hpg // 2) + pp
            lo, hi = p * LANES, (p + 1) * LANES
            xs = u[:, lo:hi]
            xs16 = xs.astype(BF16)
            yd = []
            for hh in range(2):
                h = 2 * p + hh
                seg = cum[:, h:h + 1] - cum_t[h:h + 1, :]
                decay = jnp.exp(jnp.where(tril, seg, -jnp.inf))
                mm = cb * decay * dt_t[h:h + 1, :]
                yd.append(_dot(mm.astype(BF16), xs16))
            y = jnp.where(lane_lo, yd[0], yd[1])
            h_pair = h_ref[pl.ds(lo, LANES), :]
            y = y + _dot_nt(cm16, h_pair.astype(BF16)) * e_cumw[:, lo:hi]
            y = y + dl_ref[:, lo:hi] * xs
            cs = _dot_tn((xs * w_in_state[:, lo:hi]).astype(BF16), bm16)
            dec = jnp.where(row_lo, e_last[:, 2 * p:2 * p + 1], e_last[:, 2 * p + 1:2 * p + 2])
            h_ref[pl.ds(lo, LANES), :] = dec * h_pair + cs
            yg = y * _silu(z_ref[:, lo:hi])
            yg_ref[:, lo:hi] = yg
            ssq = ssq + jnp.sum(yg * yg, axis=-1, keepdims=True)
    scale = lax.rsqrt(ssq * (1.0 / SSD_INNER) + EPS)
    y_ref[...] = (yg_ref[...] * scale * gn_ref[...]).astype(BF16)
    hl_ref[...] = h_ref[...]


def ssd_mixer(z, xbc, dt, buf, h0, cw, cb, alog, dlane, gn, sel, *, nb, seq, valid):
    q = SSD_CHUNK
    nc = seq // q
    group = math.gcd(nb, SSD_SEQS_PER_STEP)
    seqs = lambda a: a.reshape(nb, seq, a.shape[-1])
    tok = lambda w: pl.BlockSpec((group, q, w), lambda b, c: (b, c, 0))
    per_seq = lambda r, w: pl.BlockSpec((group, r, w), lambda b, c: (b, 0, 0))
    y, nbuf, hl = pl.pallas_call(
        functools.partial(_ssd_kernel, nc=nc, valid=valid, group=group),
        out_shape=(jax.ShapeDtypeStruct((nb, seq, SSD_INNER), BF16),
                   jax.ShapeDtypeStruct((nb, SSD_CONV - 1, SSD_CONV_DIM), F32),
                   jax.ShapeDtypeStruct((nb, SSD_INNER, SSD_STATE), F32)),
        grid=(nb // group, nc),
        in_specs=[tok(SSD_INNER), tok(SSD_CONV_DIM), tok(LANES),
                  per_seq(SSD_CONV - 1, SSD_CONV_DIM), per_seq(SSD_INNER, SSD_STATE),
                  _resident(cw.shape), _resident(cb.shape), _resident(alog.shape), _resident(dlane.shape),
                  _resident(gn.shape), _resident(sel.shape)],
        out_specs=(tok(SSD_INNER), per_seq(SSD_CONV - 1, SSD_CONV_DIM), per_seq(SSD_INNER, SSD_STATE)),
        scratch_shapes=[pltpu.VMEM((group, q + _SSD_PAD, SSD_CONV_DIM), F32),
                        pltpu.VMEM((group, SSD_INNER, SSD_STATE), F32),
                        pltpu.VMEM((group, q, SSD_INNER), F32)],
        compiler_params=_params("parallel", "arbitrary"),
        name="ssd_mixer",
    )(seqs(z), seqs(xbc), seqs(dt), buf, h0, cw, cb, alog, dlane, gn, sel)
    return y.reshape(nb * seq, SSD_INNER), nbuf, hl


def _cc_kernel(u_ref, buf_ref, w_ref, b_ref, g_ref, bb_ref, y_ref, nbuf_ref, ext_ref, sh_ref, *, tt, nt):
    j = pl.program_id(1)
    hist = CC_WIDTH - 1
    pad = 32
    sub = 8
    rb = min(tt, 32)

    @pl.when(j == 0)
    def _():
        ext_ref[pl.ds(pad - hist, hist), :] = buf_ref[0]

    ext_ref[pl.ds(pad, tt), :] = u_ref[...]
    for s in range(1, sub):
        sh_ref[s - 1] = ext_ref[pl.ds(s, pad + tt - sub), :]
    for r0 in range(0, tt, rb):
        acc = jnp.zeros((rb, CC_CH), F32) + b_ref[...]
        for k in range(CC_WIDTH):
            off = pad - hist + k
            base, s = off - off % sub + r0, off % sub
            rows = ext_ref[pl.ds(base, rb), :] if s == 0 else sh_ref[s - 1, pl.ds(base, rb), :]
            acc = acc + rows * w_ref[pl.ds(k, 1), :]
        mu = jnp.mean(acc, axis=-1, keepdims=True)
        xc = acc - mu
        v = xc * lax.rsqrt(jnp.mean(xc * xc, axis=-1, keepdims=True) + EPS) * g_ref[...] + bb_ref[...]
        y_ref[pl.ds(r0, rb), :] = _silu(v).astype(BF16)

    @pl.when(j == nt - 1)
    def _():
        nbuf_ref[0] = ext_ref[pl.ds(pad + tt - hist, hist), :]

    if nt > 1:
        ext_ref[pl.ds(pad - hist, hist), :] = ext_ref[pl.ds(pad + tt - hist, hist), :]


def cc_mixer(u, buf, w, b, g, bb, *, nb, seq):
    tt = min(seq, 128)
    nt = seq // tt
    assert nt == 1 or tt >= CC_WIDTH - 1
    return pl.pallas_call(
        functools.partial(_cc_kernel, tt=tt, nt=nt),
        out_shape=(jax.ShapeDtypeStruct((nb * seq, CC_CH), BF16),
                   jax.ShapeDtypeStruct((nb, CC_WIDTH - 1, CC_CH), F32)),
        grid=(nb, nt),
        in_specs=[pl.BlockSpec((tt, CC_CH), lambda i, j: (i * nt + j, 0)),
                  pl.BlockSpec((1, CC_WIDTH - 1, CC_CH), lambda i, j: (i, 0, 0)),
                  _resident(w.shape), _resident(b.shape), _resident(g.shape), _resident(bb.shape)],
        out_specs=(pl.BlockSpec((tt, CC_CH), lambda i, j: (i * nt + j, 0)),
                   pl.BlockSpec((1, CC_WIDTH - 1, CC_CH), lambda i, j: (i, 0, 0))),
        scratch_shapes=[pltpu.VMEM((32 + tt, CC_CH), F32), pltpu.VMEM((7, 24 + tt, CC_CH), F32)],
        compiler_params=_params("parallel", "arbitrary"),
        name="cc_mixer",
    )(u, buf, w, b, g, bb)


def _fox_prompt_kernel(q_ref, kt_ref, vt_ref, ct_ref, o_ref, k16_ref, v16_ref, *, tq, nq, npair):
    hg = pl.program_id(1)
    i = pl.program_id(2)

    @pl.when(i == 0)
    def _():
        k16_ref[...] = kt_ref[0].astype(BF16)
        v16_ref[...] = vt_ref[0].astype(BF16)

    def attend(ii):
        w = (ii + 1) * tq
        lane = lax.broadcasted_iota(jnp.int32, (tq, LANES), 1)
        keep = (lax.broadcasted_iota(jnp.int32, (tq, tq), 1) <= lax.broadcasted_iota(jnp.int32, (tq, tq), 0))
        ss = []
        for pp in range(npair):
            q = q_ref[:, pp * LANES:(pp + 1) * LANES]
            k16 = k16_ref[pp * LANES:(pp + 1) * LANES, :w]
            for hh in range(2):
                qh = jnp.where((lane >= hh * FOX_HEAD_DIM) & (lane < (hh + 1) * FOX_HEAD_DIM), q, jnp.zeros_like(q))
                s = _dot(qh, k16) - ct_ref[0, pl.ds(2 * (hg * npair + pp) + hh, 1), :w]
                tail = jnp.where(keep, s[:, w - tq:], -jnp.inf)
                ss.append(tail if ii == 0 else jnp.concatenate([s[:, :w - tq], tail], axis=1))
        ps = [jnp.exp(s - jnp.max(s, axis=-1, keepdims=True)) for s in ss]
        for pp in range(npair):
            v16 = v16_ref[pp * LANES:(pp + 1) * LANES, :w]
            o0, o1 = [_dot_nt(p.astype(BF16), v16) / jnp.sum(p, axis=-1, keepdims=True)
                      for p in ps[2 * pp:2 * pp + 2]]
            o_ref[:, pp * LANES:(pp + 1) * LANES] = jnp.where(lane < FOX_HEAD_DIM, o0, o1).astype(BF16)

    for ii in range(nq):
        pl.when(i == ii)(functools.partial(attend, ii))


def fox_prompt(q, kt, vt, ct, *, nb, seq):
    tq = min(seq, 256)
    nq = seq // tq
    npair = FOX_PAIRS_PER_STEP
    wd = npair * LANES
    return pl.pallas_call(
        functools.partial(_fox_prompt_kernel, tq=tq, nq=nq, npair=npair),
        out_shape=jax.ShapeDtypeStruct((nb * seq, FOX_WIDTH), BF16),
        grid=(nb, FOX_WIDTH // wd, nq),
        in_specs=[pl.BlockSpec((tq, wd), lambda b, h, i: (b * nq + i, h)),
                  pl.BlockSpec((1, wd, seq), lambda b, h, i: (b, h, 0)),
                  pl.BlockSpec((1, wd, seq), lambda b, h, i: (b, h, 0)),
                  pl.BlockSpec((1, HEAD_ROWS, seq), lambda b, h, i: (b, 0, 0))],
        out_specs=pl.BlockSpec((tq, wd), lambda b, h, i: (b * nq + i, h)),
        scratch_shapes=[pltpu.VMEM((wd, seq), BF16), pltpu.VMEM((wd, seq), BF16)],
        compiler_params=_params("parallel", "parallel", "arbitrary"),
        name="fox_prompt",
    )(q, kt, vt, ct)


def _fox_sample_kernel(pt_ref, qbd_ref, knew_ref, vnew_ref, lfnew_ref, *refs, layer, npg, nsteps, tnew):
    del pt_ref, layer
    k_refs = refs[:npg]
    v_refs = refs[npg:2 * npg]
    f_refs = refs[2 * npg:3 * npg]
    o_ref = refs[3 * npg]
    m_ref, l_ref, acc_ref, carry_ref, kcat_ref, vcat_ref = refs[3 * npg + 1:]
    j = pl.program_id(1)
    rows = 8 * HEAD_ROWS

    @pl.when(j == 0)
    def _():
        m_ref[...] = jnp.full(m_ref.shape, -1e30, F32)
        l_ref[...] = jnp.zeros(l_ref.shape, F32)
        acc_ref[...] = jnp.zeros(acc_ref.shape, F32)
        carry_ref[...] = jnp.zeros(carry_ref.shape, F32)

    qbd = qbd_ref[0]
    lane = lax.broadcasted_iota(jnp.int32, (HEAD_ROWS, LANES), 1)

    def prefix(lf):
        c = lf
        sh = 1
        while sh < LANES:
            c = c + jnp.where(lane >= sh, pltpu.roll(c, sh, axis=1), 0.0)
            sh *= 2
        return c

    def attend(kt16, vt16, lfs, keep):
        s_all = _dot(qbd, kt16)
        carry = carry_ref[...]
        ss = []
        for r, lf in enumerate(lfs):
            cum = prefix(lf) + carry
            carry = carry + jnp.sum(lf, axis=-1, keepdims=True)
            s = s_all[:, r * LANES:(r + 1) * LANES] - jnp.tile(cum, (rows // HEAD_ROWS, 1))
            if keep is not None:
                s = jnp.where(keep, s, -jnp.inf)
            ss.append(s)
        carry_ref[...] = carry
        m_cur = ss[0]
        for s in ss[1:]:
            m_cur = jnp.maximum(m_cur, s)
        m_prev = m_ref[...]
        m_new = jnp.maximum(m_prev, jnp.max(m_cur, axis=-1, keepdims=True))
        alpha = jnp.exp(m_prev - m_new)
        ps = [jnp.exp(s - m_new) for s in ss]
        p_sum = ps[0]
        for p in ps[1:]:
            p_sum = p_sum + p
        l_ref[...] = alpha * l_ref[...] + jnp.sum(p_sum, axis=-1, keepdims=True)
        m_ref[...] = m_new
        p16 = jnp.concatenate([p.astype(BF16) for p in ps], axis=1) if len(ps) > 1 else ps[0].astype(BF16)
        pv = _dot_nt(vt16, p16)
        acc_ref[...] = acc_ref[...] * alpha.T[0:1, :] + pv

    zpad = jnp.zeros((HEAD_ROWS - FOX_HEADS, LANES), F32)
    for r in range(npg):
        kcat_ref[:, r * LANES:(r + 1) * LANES] = k_refs[r][0, 0].astype(BF16)
        vcat_ref[:, r * LANES:(r + 1) * LANES] = v_refs[r][0, 0].astype(BF16)
    attend(kcat_ref[...], vcat_ref[...],
           [jnp.concatenate([f_refs[r][0], zpad], axis=0) for r in range(npg)], None)

    @pl.when(j == nsteps - 1)
    def _():
        srow = lax.broadcasted_iota(jnp.int32, (rows, LANES), 0)
        scol = lax.broadcasted_iota(jnp.int32, (rows, LANES), 1)
        keep = (scol <= srow // HEAD_ROWS) & (scol < tnew)
        lf_new = jnp.where(lane < tnew, lfnew_ref[0], 0.0)
        attend(knew_ref[0].astype(BF16), vnew_ref[0].astype(BF16), [lf_new], keep)
        o_t = acc_ref[...] * (1.0 / l_ref[...]).T[0:1, :]
        o = o_t.T
        orow = lax.broadcasted_iota(jnp.int32, (rows, FOX_WIDTH), 0)
        ocol = lax.broadcasted_iota(jnp.int32, (rows, FOX_WIDTH), 1)
        o = jnp.where((orow % HEAD_ROWS) == (ocol // FOX_HEAD_DIM), o, 0.0)
        pick = (lax.broadcasted_iota(jnp.int32, (8, rows), 1) // HEAD_ROWS
                == lax.broadcasted_iota(jnp.int32, (8, rows), 0)).astype(F32)
        o_ref[0] = _dot_hi(pick, o).astype(BF16)


def fox_sample(qbd, knew, vnew, lfnew, kcache, vcache, fcache, page_table, *, layer, tnew):
    nb, n_pages = page_table.shape
    npg = min(PAGES_PER_STEP, n_pages)
    nsteps = n_pages // npg
    assert tnew <= 8

    def page_spec(r, shape, kind):
        if kind == "kv":
            return pl.BlockSpec((1, 1, FOX_WIDTH, PAGE_SIZE), lambda b, j, pt: (layer, pt[b, j * npg + r], 0, 0))
        return pl.BlockSpec((1, FOX_HEADS, PAGE_SIZE), lambda b, j, pt: (layer, 0, pt[b, j * npg + r]))

    per_seq = lambda shape: pl.BlockSpec((1,) + shape, lambda b, j, pt: (b, 0, 0))
    in_specs = [per_seq((8 * HEAD_ROWS, FOX_WIDTH)), per_seq((FOX_WIDTH, LANES)), per_seq((FOX_WIDTH, LANES)),
                per_seq((HEAD_ROWS, LANES))]
    in_specs += [page_spec(r, None, "kv") for r in range(npg)]
    in_specs += [page_spec(r, None, "kv") for r in range(npg)]
    in_specs += [page_spec(r, None, "f") for r in range(npg)]
    grid_spec = pltpu.PrefetchScalarGridSpec(
        num_scalar_prefetch=1,
        grid=(nb, nsteps),
        in_specs=in_specs,
        out_specs=pl.BlockSpec((1, 8, FOX_WIDTH), lambda b, j, pt: (b, 0, 0)),
        scratch_shapes=[pltpu.VMEM((8 * HEAD_ROWS, LANES), F32), pltpu.VMEM((8 * HEAD_ROWS, LANES), F32),
                        pltpu.VMEM((FOX_WIDTH, LANES), F32), pltpu.VMEM((HEAD_ROWS, LANES), F32),
                        pltpu.VMEM((FOX_WIDTH, npg * PAGE_SIZE), BF16),
                        pltpu.VMEM((FOX_WIDTH, npg * PAGE_SIZE), BF16)],
    )
    return pl.pallas_call(
        functools.partial(_fox_sample_kernel, layer=layer, npg=npg, nsteps=nsteps, tnew=tnew),
        out_shape=jax.ShapeDtypeStruct((nb, 8, FOX_WIDTH), BF16),
        grid_spec=grid_spec,
        compiler_params=_params("parallel", "arbitrary"),
        name="fox_sample",
    )(page_table, qbd, knew, vnew, lfnew, *([kcache] * npg), *([vcache] * npg), *([fcache] * npg))


def _outproj_kernel(ys_ref, yc_ref, yf_ref, w_ref, x_ref, g_ref, xo_ref, no_ref):
    a = SSD_INNER
    b = SSD_INNER + CC_CH
    acc = _dot(ys_ref[...], w_ref[0, 0:a, :])
    acc = acc + _dot(yc_ref[...], w_ref[0, a:b, :])
    acc = acc + _dot(yf_ref[...], w_ref[0, b:, :])
    xn = x_ref[...] + acc
    xo_ref[...] = xn
    no_ref[...] = _rms_rows(xn, g_ref[...]).astype(BF16)


def out_proj(ys, yc, yf, w, x, g, *, layer):
    m, d = x.shape
    tm = min(m, 512)
    row = lambda wd: pl.BlockSpec((tm, wd), lambda i: (i, 0))
    return pl.pallas_call(
        _outproj_kernel,
        out_shape=(jax.ShapeDtypeStruct((m, d), F32), jax.ShapeDtypeStruct((m, d), BF16)),
        grid=(m // tm,),
        in_specs=[row(SSD_INNER), row(CC_CH), row(FOX_WIDTH), _layer_block(w.shape[1:], layer), row(d),
                  _resident(g.shape)],
        out_specs=(row(d), row(d)),
        compiler_params=_params("parallel"),
        name="out_proj",
    )(ys, yc, yf, w, x, g)


def _memkv_kernel(m_ref, gm_ref, wk_ref, wv_ref, gk_ref, k_ref, v_ref):
    mn = _rms_rows(m_ref[...], gm_ref[...]).astype(BF16)
    k = _dot(mn, wk_ref[0])
    for h in range(XA_HEADS):
        lo, hi = h * XA_HEAD_DIM, (h + 1) * XA_HEAD_DIM
        k_ref[:, lo:hi] = _rms_rows(k[:, lo:hi], gk_ref[...])
    v_ref[...] = _dot(mn, wv_ref[0])


def mem_kv(mem, gm, wk, wv, gk, *, layer):
    m, d = mem.shape
    tm = min(m, 256)
    row = lambda wd: pl.BlockSpec((tm, wd), lambda i: (i, 0))
    return pl.pallas_call(
        _memkv_kernel,
        out_shape=(jax.ShapeDtypeStruct((m, XA_WIDTH), F32), jax.ShapeDtypeStruct((m, XA_WIDTH), F32)),
        grid=(m // tm,),
        in_specs=[row(d), _resident(gm.shape), _layer_block(wk.shape[1:], layer), _layer_block(wv.shape[1:], layer),
                  _resident(gk.shape)],
        out_specs=(row(XA_WIDTH), row(XA_WIDTH)),
        compiler_params=_params("parallel"),
        name="mem_kv",
    )(mem, gm, wk, wv, gk)


def _xattn_kernel(n_ref, x_ref, wq_ref, qg_ref, mk_ref, mv_ref, wo_ref, g_ref, xo_ref, no_ref):
    q = _dot(n_ref[...], wq_ref[0])
    mk = mk_ref[0].astype(BF16)
    mv = mv_ref[0].astype(BF16)
    outs = []
    for h in range(XA_HEADS):
        lo, hi = h * XA_HEAD_DIM, (h + 1) * XA_HEAD_DIM
        qh = _rms_rows(q[:, lo:hi], qg_ref[...]).astype(BF16)
        s = _dot_nt(qh, mk[:, lo:hi]) * (XA_HEAD_DIM ** -0.5)
        s = s - jnp.max(s, axis=-1, keepdims=True)
        e = jnp.exp(s)
        p = e / jnp.sum(e, axis=-1, keepdims=True)
        outs.append(_dot(p.astype(BF16), mv[:, lo:hi]))
    o = jnp.concatenate(outs, axis=-1).astype(BF16)
    xn = x_ref[...] + _dot(o, wo_ref[0])
    xo_ref[...] = xn
    no_ref[...] = _rms_rows(xn, g_ref[...]).astype(BF16)


def cross_attn(n, x, wq, qg, mk, mv, wo, g, *, layer, seq):
    m, d = x.shape
    tm = min(seq, 512)
    per = seq // tm
    mem = mk.shape[1]
    row = lambda wd: pl.BlockSpec((tm, wd), lambda i: (i, 0))
    kv = pl.BlockSpec((1, mem, XA_WIDTH), lambda i: (i // per, 0, 0))
    return pl.pallas_call(
        _xattn_kernel,
        out_shape=(jax.ShapeDtypeStruct((m, d), F32), jax.ShapeDtypeStruct((m, d), BF16)),
        grid=(m // tm,),
        in_specs=[row(d), row(d), _layer_block(wq.shape[1:], layer), _resident(qg.shape), kv, kv,
                  _layer_block(wo.shape[1:], layer), _resident(g.shape)],
        out_specs=(row(d), row(d)),
        compiler_params=_params("parallel"),
        name="cross_attn",
    )(n, x, wq, qg, mk, mv, wo, g)


def _mix_xattn_kernel(ys_ref, yc_ref, yf_ref, w_ref, x_ref, gx_ref, wq_ref, qg_ref, mk_ref, mv_ref, wo_ref, g_ref,
                      xo_ref, no_ref):
    a = SSD_INNER
    b = SSD_INNER + CC_CH
    acc = _dot(ys_ref[...], w_ref[0, 0:a, :])
    acc = acc + _dot(yc_ref[...], w_ref[0, a:b, :])
    acc = acc + _dot(yf_ref[...], w_ref[0, b:, :])
    x1 = x_ref[...] + acc
    n1 = _rms_rows(x1, gx_ref[...]).astype(BF16)
    q = _dot(n1, wq_ref[0])
    mk = mk_ref[0].astype(BF16)
    mv = mv_ref[0].astype(BF16)
    outs = []
    for h in range(XA_HEADS):
        lo, hi = h * XA_HEAD_DIM, (h + 1) * XA_HEAD_DIM
        qh = _rms_rows(q[:, lo:hi], qg_ref[...]).astype(BF16)
        s = _dot_nt(qh, mk[:, lo:hi]) * (XA_HEAD_DIM ** -0.5)
        s = s - jnp.max(s, axis=-1, keepdims=True)
        e = jnp.exp(s)
        p = e / jnp.sum(e, axis=-1, keepdims=True)
        outs.append(_dot(p.astype(BF16), mv[:, lo:hi]))
    o = jnp.concatenate(outs, axis=-1).astype(BF16)
    x2 = x1 + _dot(o, wo_ref[0])
    xo_ref[...] = x2
    no_ref[...] = _rms_rows(x2, g_ref[...]).astype(BF16)


def mix_xattn(ys, yc, yf, w, x, gx, wq, qg, mk, mv, wo, g, *, layer, seq):
    m, d = x.shape
    tm = min(seq, 512)
    per = seq // tm
    mem = mk.shape[1]
    row = lambda wd: pl.BlockSpec((tm, wd), lambda i: (i, 0))
    kv = pl.BlockSpec((1, mem, XA_WIDTH), lambda i: (i // per, 0, 0))
    return pl.pallas_call(
        _mix_xattn_kernel,
        out_shape=(jax.ShapeDtypeStruct((m, d), F32), jax.ShapeDtypeStruct((m, d), BF16)),
        grid=(m // tm,),
        in_specs=[row(SSD_INNER), row(CC_CH), row(FOX_WIDTH), _layer_block(w.shape[1:], layer), row(d),
                  _resident(gx.shape), _layer_block(wq.shape[1:], layer), _resident(qg.shape), kv, kv,
                  _layer_block(wo.shape[1:], layer), _resident(g.shape)],
        out_specs=(row(d), row(d)),
        compiler_params=_params("parallel"),
        name="mix_xattn",
    )(ys, yc, yf, w, x, gx, wq, qg, mk, mv, wo, g)


def _ffn_kernel(n_ref, x_ref, wg_ref, wu_ref, wd_ref, g_ref, xo_ref, no_ref, acc_ref, *, nf):
    f = pl.program_id(1)

    @pl.when(f == 0)
    def _():
        acc_ref[...] = jnp.zeros_like(acc_ref)

    n = n_ref[...]
    a = (_silu(_dot(n, wg_ref[0])) * _dot(n, wu_ref[0])).astype(BF16)
    acc_ref[...] += _dot(a, wd_ref[0])

    @pl.when(f == nf - 1)
    def _():
        xn = x_ref[...] + acc_ref[...]
        xo_ref[...] = xn
        no_ref[...] = _rms_rows(xn, g_ref[...]).astype(BF16)


def ffn(n, x, wg, wu, wd, g, *, layer):
    m, d = x.shape
    tm = min(m, 512)
    tf = FFN_TILE
    nf = wg.shape[2] // tf
    row = lambda wdt: pl.BlockSpec((tm, wdt), lambda i, f: (i, 0))
    return pl.pallas_call(
        functools.partial(_ffn_kernel, nf=nf),
        out_shape=(jax.ShapeDtypeStruct((m, d), F32), jax.ShapeDtypeStruct((m, d), BF16)),
        grid=(m // tm, nf),
        in_specs=[row(d), row(d),
                  pl.BlockSpec((1, d, tf), lambda i, f: (layer, 0, f)),
                  pl.BlockSpec((1, d, tf), lambda i, f: (layer, 0, f)),
                  pl.BlockSpec((1, tf, d), lambda i, f: (layer, f, 0)), _resident(g.shape)],
        out_specs=(row(d), row(d)),
        scratch_shapes=[pltpu.VMEM((tm, d), F32)],
        compiler_params=_params("parallel", "arbitrary"),
        name="ffn",
    )(n, x, wg, wu, wd, g)


def _head_selector():
    r = jnp.arange(LANES)[:, None]
    c = jnp.arange(FOX_WIDTH)[None, :]
    return (r == c // FOX_HEAD_DIM).astype(BF16)


def _pad_lanes(v, width=LANES):
    return jnp.pad(v, (0, width - v.shape[0]))[None, :]


def kernel(x_prompt, x_sample, mem_prompt, cache_fox_k, cache_fox_v, cache_fox_logf, page_table, state_ssd, state_ssd_conv, state_cc_conv, cache_mem_k, cache_mem_v, norm_mix, w_in, ssd_conv_w, ssd_conv_b, ssd_dt_bias, ssd_a_log, ssd_d, ssd_norm, cc_dw_w, cc_dw_b, cc_ln_g, cc_ln_b, fox_q_norm, fox_k_norm, fox_fg_bias, w_out, norm_xa, norm_mem, xa_wq, xa_wk, xa_wv, xa_q_norm, xa_k_norm, xa_wo, norm_ffn, ffn_wg, ffn_wu, ffn_wd):
    depth = norm_mix.shape[0]
    bp, lp, d = x_prompt.shape
    bs, ls, _ = x_sample.shape
    mp, ms = bp * lp, bs * ls
    n_pool = cache_fox_k.shape[1]
    mem_len = mem_prompt.shape[1]
    q = SSD_CHUNK

    sel = _head_selector()
    selt = sel.T
    kcache = jnp.transpose(cache_fox_k, (0, 1, 3, 4, 2)).reshape(depth, n_pool, FOX_WIDTH, PAGE_SIZE)
    vcache = jnp.transpose(cache_fox_v, (0, 1, 3, 4, 2)).reshape(depth, n_pool, FOX_WIDTH, PAGE_SIZE)
    fcache = jnp.transpose(cache_fox_logf, (0, 3, 1, 2)).reshape(depth, FOX_HEADS, n_pool * PAGE_SIZE)
    w_in_t = jnp.transpose(w_in, (0, 2, 1))
    w_a = w_in_t.astype(BF16)
    w_b = w_a[:, _OFF_GLU:]
    w_o = w_out.astype(BF16)
    wxq, wxk, wxv, wxo = (w.astype(BF16) for w in (xa_wq, xa_wk, xa_wv, xa_wo))
    wg, wu, wd = ffn_wg.astype(BF16), ffn_wu.astype(BF16), ffn_wd.astype(BF16)

    xp = x_prompt.reshape(mp, d)
    xs = x_sample.reshape(ms, d)
    mem = mem_prompt.reshape(bp * mem_len, d)
    np_ = rmsnorm_bf16(xp, norm_mix[0][None, :])
    ns_ = rmsnorm_bf16(xs, norm_mix[0][None, :])

    zeros_ssd_buf = jnp.zeros((bp, SSD_CONV - 1, SSD_CONV_DIM), F32)
    zeros_ssd_h = jnp.zeros((bp, SSD_INNER, SSD_STATE), F32)
    zeros_cc_buf = jnp.zeros((bp, CC_WIDTH - 1, CC_CH), F32)

    outs = [[] for _ in range(14)]
    for i in range(depth):
        bd = _pad_lanes(ssd_dt_bias[i])
        qg = (jnp.tile(fox_q_norm[i], FOX_HEADS) * (FOX_HEAD_DIM ** -0.5))[None, :]
        gk = jnp.broadcast_to(jnp.tile(fox_k_norm[i], FOX_HEADS)[:, None], (FOX_WIDTH, LANES))
        bfg = jnp.broadcast_to(jnp.pad(fox_fg_bias[i], (0, HEAD_ROWS - FOX_HEADS))[:, None], (HEAD_ROWS, LANES))
        alog = _pad_lanes(ssd_a_log[i])
        dlane = jnp.repeat(ssd_d[i], SSD_HEAD_DIM)[None, :]
        gn = ssd_norm[i][None, :]
        g_next = norm_mix[i + 1][None, :] if i + 1 < depth else jnp.ones((1, d), F32)
        conv_args = (ssd_conv_w[i], ssd_conv_b[i][None, :], alog, dlane, gn, sel)
        cc_args = (cc_dw_w[i], cc_dw_b[i][None, :], cc_ln_g[i][None, :], cc_ln_b[i][None, :])

        mk, mv = mem_kv(mem, norm_mem[i][None, :], wxk, wxv, xa_k_norm[i][None, :], layer=i)
        z, xbc, dt = proj_ssd(np_, w_a, bd, layer=i)
        ucc, qn = proj_ccq(np_, w_b, qg, sel, selt, layer=i)
        kt, vt, lft, cumt = proj_t(np_, w_b, gk, bfg, layer=i, nb=bp, seq=lp, seg=lp)
        y_ssd, ssd_buf, ssd_h = ssd_mixer(z, xbc, dt, zeros_ssd_buf, zeros_ssd_h, *conv_args,
                                          nb=bp, seq=lp, valid=lp)
        y_cc, cc_buf = cc_mixer(ucc, zeros_cc_buf, *cc_args, nb=bp, seq=lp)
        y_fox = fox_prompt(qn, kt, vt, cumt, nb=bp, seq=lp)
        xp, nff = mix_xattn(y_ssd, y_cc, y_fox, w_o, xp, norm_xa[i][None, :], wxq, xa_q_norm[i][None, :],
                            mk.reshape(bp, mem_len, XA_WIDTH), mv.reshape(bp, mem_len, XA_WIDTH), wxo,
                            norm_ffn[i][None, :], layer=i, seq=lp)
        xp, np_ = ffn(nff, xp, wg, wu, wd, g_next, layer=i)
        outs[0].append(jnp.transpose(kt.reshape(bp, FOX_HEADS, FOX_HEAD_DIM, lp), (0, 3, 1, 2)))
        outs[1].append(jnp.transpose(vt.reshape(bp, FOX_HEADS, FOX_HEAD_DIM, lp), (0, 3, 1, 2)))
        outs[2].append(jnp.transpose(lft[:, :FOX_HEADS, :], (0, 2, 1)))
        outs[3].append(ssd_h.reshape(bp, SSD_HEADS, SSD_HEAD_DIM, SSD_STATE))
        outs[4].append(ssd_buf)
        outs[5].append(cc_buf)
        outs[6].append(mk.reshape(bp, mem_len, XA_HEADS, XA_HEAD_DIM))
        outs[7].append(mv.reshape(bp, mem_len, XA_HEADS, XA_HEAD_DIM))

        z, xbc, dt = proj_ssd(ns_, w_a, bd, layer=i)
        ucc, qn = proj_ccq(ns_, w_b, qg, sel, selt, layer=i)
        kt, vt, lft, _ = proj_t(ns_, w_b, gk, bfg, layer=i, nb=1, seq=ms, seg=ls)
        pad_tok = lambda a: jnp.pad(a.reshape(bs, ls, a.shape[-1]), ((0, 0), (0, q - ls), (0, 0))).reshape(bs * q, a.shape[-1])
        y_ssd, ssd_buf, ssd_h = ssd_mixer(pad_tok(z), pad_tok(xbc), pad_tok(dt), state_ssd_conv[i],
                                          state_ssd[i].reshape(bs, SSD_INNER, SSD_STATE), *conv_args,
                                          nb=bs, seq=q, valid=ls)
        y_ssd = y_ssd.reshape(bs, q, SSD_INNER)[:, :ls].reshape(ms, SSD_INNER)
        y_cc, cc_buf = cc_mixer(ucc, state_cc_conv[i], *cc_args, nb=bs, seq=ls)
        q4 = qn.reshape(bs, ls, FOX_HEADS, 1, FOX_HEAD_DIM)
        eye = jnp.eye(FOX_HEADS, dtype=BF16)[None, None, :, :, None]
        qbd = (q4 * eye).reshape(bs, ls, FOX_HEADS, FOX_WIDTH)
        qbd = jnp.pad(qbd, ((0, 0), (0, 8 - ls), (0, HEAD_ROWS - FOX_HEADS), (0, 0))).reshape(bs, 8 * HEAD_ROWS, FOX_WIDTH)
        to_seq = lambda a: jnp.pad(jnp.transpose(a[0].reshape(a.shape[1], bs, ls), (1, 0, 2)),
                                   ((0, 0), (0, 0), (0, LANES - ls)))
        y_fox = fox_sample(qbd, to_seq(kt), to_seq(vt), to_seq(lft), kcache, vcache, fcache, page_table,
                           layer=i, tnew=ls)
        y_fox = y_fox[:, :ls].reshape(ms, FOX_WIDTH)
        xs, nxa = out_proj(y_ssd, y_cc, y_fox, w_o, xs, norm_xa[i][None, :], layer=i)
        xs, nff = cross_attn(nxa, xs, wxq, xa_q_norm[i][None, :], cache_mem_k[i].reshape(bs, mem_len, XA_WIDTH),
                             cache_mem_v[i].reshape(bs, mem_len, XA_WIDTH), wxo, norm_ffn[i][None, :], layer=i,
                             seq=ls)
        xs, ns_ = ffn(nff, xs, wg, wu, wd, g_next, layer=i)
        tok = lambda a, w: jnp.transpose(a[0], (1, 0)).reshape(bs, ls, w)
        outs[8].append(tok(kt, FOX_WIDTH).reshape(bs, ls, FOX_HEADS, FOX_HEAD_DIM))
        outs[9].append(tok(vt, FOX_WIDTH).reshape(bs, ls, FOX_HEADS, FOX_HEAD_DIM))
        outs[10].append(tok(lft, HEAD_ROWS)[:, :, :FOX_HEADS])
        outs[11].append(ssd_h.reshape(bs, SSD_HEADS, SSD_HEAD_DIM, SSD_STATE))
        outs[12].append(ssd_buf)
        outs[13].append(cc_buf)

    return (xp.reshape(bp, lp, d), xs.reshape(bs, ls, d)) + tuple(jnp.stack(o) for o in outs)
```
